```python
import math, functools
import jax
import jax.numpy as jnp
from jax import lax
import numpy as np

D_MODEL = 1024
BATCH = 32
SEQ = 2048
DEPTH = 4

GRID_W = 64
CTX_LEN = 256
N_MIXERS = 4
CHUNK = 128
Q_BLOCK = 128
CONV_K = 5
ROPE_BASE = 10000.0
LN_EPS = 1e-5
RMS_EPS = 1e-6
ALPHA = (2.0 * DEPTH) ** 0.25
BETA = (8.0 * DEPTH) ** -0.25

A_DI = 2 * D_MODEL
A_HEADS = 4
A_HD = A_DI // A_HEADS

B_DI = 2 * D_MODEL
B_HD = 64
B_HEADS = B_DI // B_HD
B_GROUPS = 8
B_STATE = 128
B_CONV_CH = B_DI + 2 * B_GROUPS * B_STATE

C_HD = 64
C_HEADS = D_MODEL // (2 * C_HD)

D_HEADS = 4
D_QK = D_MODEL // D_HEADS
D_V = 2 * D_MODEL // D_HEADS

P_HEADS = 8
P_NKEYS = 128
P_EXPERTS = P_NKEYS * P_NKEYS
P_DK = 256
P_TOPK = 16
P_BLOCK = 128

kernel_name = 'hybrid_mlstm_ssd_diffattn_retention_peer'

F32 = jnp.float32


def n_mixer_layers(m):
    return len(range(m, DEPTH, N_MIXERS))


def layer_norm(x, g, b):
    xf = x.astype(F32)
    mu = jnp.mean(xf, axis=-1, keepdims=True)
    var = jnp.mean(jnp.square(xf - mu), axis=-1, keepdims=True)
    return ((xf - mu) * lax.rsqrt(var + LN_EPS)).astype(x.dtype) * g + b


def rms_norm(x, g):
    xf = x.astype(F32)
    return (xf * lax.rsqrt(jnp.mean(xf * xf, axis=-1, keepdims=True) + RMS_EPS)).astype(x.dtype) * g


def group_rms_norm(x, g, groups):
    bn, t, ch = x.shape
    xf = x.astype(F32).reshape(bn, t, groups, ch // groups)
    y = xf * lax.rsqrt(jnp.mean(xf * xf, axis=-1, keepdims=True) + RMS_EPS)
    return y.reshape(bn, t, ch).astype(x.dtype) * g


def head_norm(x, g):
    xf = x.astype(F32)
    mu = jnp.mean(xf, axis=-1, keepdims=True)
    var = jnp.mean(jnp.square(xf - mu), axis=-1, keepdims=True)
    y = (xf - mu) * lax.rsqrt(var + LN_EPS)
    return y.reshape(x.shape[0], x.shape[1], -1) * g.astype(F32)


def axial_positions(n_tokens):
    rows = n_tokens // GRID_W
    row = jnp.repeat(jnp.arange(rows, dtype=jnp.int32), GRID_W)
    col = jnp.tile(jnp.arange(GRID_W, dtype=jnp.int32), rows)
    return row, col


def rope_1d(x, pos):
    half = x.shape[-1] // 2
    freqs = ROPE_BASE ** (-jnp.arange(half, dtype=F32) / half)
    ang = pos.astype(F32)[:, None] * freqs
    cos = jnp.cos(ang)[:, None, :].astype(x.dtype)
    sin = jnp.sin(ang)[:, None, :].astype(x.dtype)
    x1, x2 = x[..., :half], x[..., half:]
    return jnp.concatenate([x1 * cos - x2 * sin, x1 * sin + x2 * cos], axis=-1)


def rope_2d(x):
    row, col = axial_positions(x.shape[1])
    h = x.shape[-1] // 2
    return jnp.concatenate([rope_1d(x[..., :h], row), rope_1d(x[..., h:], col)], axis=-1)


def dw_conv(x, w, b):
    ch = x.shape[-1]
    y = lax.conv_general_dilated(x, w[:, None, :].astype(x.dtype), window_strides=(1,),
                                 padding=[(CONV_K // 2, CONV_K // 2)],
                                 dimension_numbers=('NWC', 'WIO', 'NWC'), feature_group_count=ch)
    return y + b


def to_chunks(t):
    bn, n = t.shape[:2]
    return jnp.moveaxis(t.reshape((bn, n // CHUNK, CHUNK) + t.shape[2:]), 1, 0)


def from_chunks(t):
    t = jnp.moveaxis(t, 0, 1)
    return t.reshape((t.shape[0], -1) + t.shape[3:])


def flip_time(t):
    return jnp.flip(t, axis=1)


def bidirectional_prefix_scan(scan_fwd, scan_bwd, ctx_fwd, ctx_bwd, lat_fwd, lat_bwd, init):
    y_cf, s_f = scan_fwd(ctx_fwd, init)
    y_cb, s_b = scan_bwd(tuple(flip_time(t) for t in ctx_bwd), init)
    y_lf, _ = scan_fwd(lat_fwd, s_f)
    y_lb, _ = scan_bwd(tuple(flip_time(t) for t in lat_bwd), s_b)
    return y_cf + flip_time(y_cb), y_lf + flip_time(y_lb)


def mlstm_chunk_scan(inputs, state):
    causal = jnp.tril(jnp.ones((CHUNK, CHUNK), dtype=bool))

    def step(carry, chunk):
        cmat, nvec, m = carry
        q, k, v, ig, lf = chunk
        b = jnp.swapaxes(jnp.cumsum(lf, axis=1), 1, 2)
        ig = jnp.swapaxes(ig, 1, 2)
        logw = jnp.where(causal, b[..., :, None] - b[..., None, :] + ig[..., None, :], -jnp.inf)
        inter = b + m[..., None]
        m_t = jnp.maximum(inter, jnp.max(logw, axis=-1))
        s = jnp.einsum('blhd,bshd->bhls', q, k) * jnp.exp(logw - m_t[..., None])
        a = jnp.exp(inter - m_t)
        num = jnp.einsum('bhls,bshd->bhld', s, v) + a[..., None] * jnp.einsum('bhvk,blhk->bhlv', cmat, q)
        den = jnp.sum(s, axis=-1) + a * jnp.einsum('bhk,blhk->bhl', nvec, q)
        h = num / jnp.maximum(jnp.abs(den), jnp.exp(-m_t))[..., None]
        b_last = b[..., -1]
        g = b_last[..., None] - b + ig
        m_new = jnp.maximum(b_last + m, jnp.max(g, axis=-1))
        carry_decay = jnp.exp(b_last + m - m_new)
        wg = jnp.exp(g - m_new[..., None])
        cmat = carry_decay[..., None, None] * cmat + jnp.einsum('bhs,bshv,bshk->bhvk', wg, v, k)
        nvec = carry_decay[..., None] * nvec + jnp.einsum('bhs,bshk->bhk', wg, k)
        return (cmat, nvec, m_new), jnp.swapaxes(h, 1, 2)

    state, h = lax.scan(step, state, tuple(to_chunks(t) for t in inputs))
    return from_chunks(h), state


def mlstm_mixer(h_ctx, h_lat, w_up, conv_w, conv_b, w_qk, w_v, w_gate, b_gate, norm_g, skip, w_down, need_ctx):
    def prep(h):
        bn, t, _ = h.shape
        xm, z, o_pre = jnp.split(h @ w_up, 3, axis=-1)
        xc = jax.nn.silu(dw_conv(xm, conv_w, conv_b))
        q, k = jnp.split(xc @ w_qk, 2, axis=-1)
        v = xm @ w_v
        gates = (jnp.concatenate([q, k, v], axis=-1) @ w_gate + b_gate).astype(F32).reshape(bn, t, 4, A_HEADS)
        heads = lambda a: a.astype(F32).reshape(bn, t, A_HEADS, A_HD)
        qkv = (heads(q) * A_HD ** -0.5, heads(k), heads(v))
        fwd = qkv + (gates[:, :, 0], jax.nn.log_sigmoid(gates[:, :, 1]))
        bwd = qkv + (gates[:, :, 2], jax.nn.log_sigmoid(gates[:, :, 3]))
        return fwd, bwd, (xc, z, o_pre)

    fc, bc, aux_c = prep(h_ctx)
    fl, bl, aux_l = prep(h_lat)
    bn = h_lat.shape[0]
    init = (jnp.zeros((bn, A_HEADS, A_HD, A_HD), F32), jnp.zeros((bn, A_HEADS, A_HD), F32),
            jnp.zeros((bn, A_HEADS), F32))
    hc, hl = bidirectional_prefix_scan(mlstm_chunk_scan, mlstm_chunk_scan, fc, bc, fl, bl, init)

    def out(hh, aux):
        xc, z, o_pre = aux
        bn2, t = xc.shape[:2]
        hh = jax.nn.sigmoid(o_pre.astype(F32)).reshape(bn2, t, A_HEADS, A_HD) * hh
        y = head_norm(hh, norm_g).astype(xc.dtype) + skip * xc
        return (y * jax.nn.silu(z)) @ w_down

    return (out(hc, aux_c) if need_ctx else None), out(hl, aux_l)


def ssd_chunk_scan(a_rate, inputs, state):
    causal = jnp.tril(jnp.ones((CHUNK, CHUNK), dtype=bool))

    def step(s_prev, chunk):
        x, dt, bm, cm = chunk
        acs = jnp.moveaxis(jnp.cumsum(dt * a_rate, axis=1), 1, -1)
        seg = jnp.exp(jnp.where(causal, acs[..., :, None] - acs[..., None, :], -jnp.inf))
        xdt = x * dt[..., None]
        cb = jnp.einsum('blgn,bsgn->bgls', cm, bm)
        y = (jnp.einsum('bgls,bgrls,bsgrp->blgrp', cb, seg, xdt)
             + jnp.einsum('blgn,bgrpn,bgrl->blgrp', cm, s_prev, jnp.exp(acs)))
        a_last = acs[..., -1]
        s_new = (jnp.exp(a_last)[..., None, None] * s_prev
                 + jnp.einsum('bgrs,bsgrp,bsgn->bgrpn', jnp.exp(a_last[..., None] - acs), xdt, bm))
        return s_new, y

    state, y = lax.scan(step, state, tuple(to_chunks(t) for t in inputs))
    return from_chunks(y), state


def mamba2_mixer(h_ctx, h_lat, w_in, conv_w, conv_b, dt_bias, a_log, d_skip, norm_g, w_out, need_ctx):
    r = B_HEADS // B_GROUPS
    a_rate = -jnp.exp(a_log.astype(F32)).reshape(2, B_GROUPS, r)

    def prep(h):
        bn, t, _ = h.shape
        z, xbc, dt_raw = jnp.split(h @ w_in, [B_DI, B_DI + B_CONV_CH], axis=-1)
        xbc = jax.nn.silu(dw_conv(xbc, conv_w, conv_b))
        xs, bm, cm = jnp.split(xbc, [B_DI, B_DI + B_GROUPS * B_STATE], axis=-1)
        xs = xs.astype(F32).reshape(bn, t, B_GROUPS, r, B_HD)
        bm = bm.astype(F32).reshape(bn, t, B_GROUPS, B_STATE)
        cm = cm.astype(F32).reshape(bn, t, B_GROUPS, B_STATE)
        dt = jax.nn.softplus(dt_raw.astype(F32).reshape(bn, t, 2, B_GROUPS, r)
                             + dt_bias.astype(F32).reshape(2, B_GROUPS, r))
        return (xs, dt[:, :, 0], bm, cm), (xs, dt[:, :, 1], bm, cm), (xs, z)

    fc, bc, aux_c = prep(h_ctx)
    fl, bl, aux_l = prep(h_lat)
    init = jnp.zeros((h_lat.shape[0], B_GROUPS, r, B_HD, B_STATE), F32)
    yc, yl = bidirectional_prefix_scan(functools.partial(ssd_chunk_scan, a_rate[0]),
                                       functools.partial(ssd_chunk_scan, a_rate[1]), fc, bc, fl, bl, init)

    def out(y, aux):
        xs, z = aux
        bn, t = z.shape[:2]
        y = y + d_skip.astype(F32).reshape(B_GROUPS, r)[..., None] * xs
        y = y.reshape(bn, t, B_DI).astype(z.dtype) * jax.nn.silu(z)
        return group_rms_norm(y, norm_g, B_GROUPS) @ w_out

    return (out(yc, aux_c) if need_ctx else None), out(yl, aux_l)


def diff_attention_mixer(h_ctx, h_lat, w_qkv, lam_vecs, norm_g, w_out, layer_idx, need_ctx):
    lam_init = 0.8 - 0.6 * math.exp(-0.3 * layer_idx)
    lv = lam_vecs.astype(F32)
    lam = jnp.exp(jnp.sum(lv[0] * lv[1])) - jnp.exp(jnp.sum(lv[2] * lv[3])) + lam_init

    def prep(h):
        bn, t, _ = h.shape
        q, k, v = jnp.split(h @ w_qkv, 3, axis=-1)
        return (q.reshape(bn, t, 2 * C_HEADS, C_HD), k.reshape(bn, t, 2 * C_HEADS, C_HD),
                v.reshape(bn, t, C_HEADS, 2 * C_HD))

    qc, kc, vc = prep(h_ctx)
    ql, kl, vl = prep(h_lat)
    ql, kl = rope_2d(ql), rope_2d(kl)
    k_all = jnp.concatenate([kc, kl], axis=1)
    v_all = jnp.concatenate([vc, vl], axis=1)

    def attend(q, k, v):
        s = jnp.einsum('bqhd,bkhd->bhqk', q, k).astype(F32) * C_HD ** -0.5
        p = jax.nn.softmax(s, axis=-1)
        p = p.reshape(p.shape[0], C_HEADS, 2, p.shape[2], p.shape[3])
        p = p[:, :, 0] - lam * p[:, :, 1]
        return jnp.einsum('bhqk,bkhe->bqhe', p.astype(v.dtype), v)

    bn, t = ql.shape[:2]
    q_blocks = jnp.moveaxis(ql.reshape(bn, t // Q_BLOCK, Q_BLOCK, 2 * C_HEADS, C_HD), 1, 0)
    o_l = lax.map(lambda qb: attend(qb, k_all, v_all), q_blocks)
    o_l = jnp.moveaxis(o_l, 0, 1).reshape(bn, t, C_HEADS, 2 * C_HD)

    def out(o):
        bn2, t2 = o.shape[:2]
        o = rms_norm(o, norm_g) * (1.0 - lam_init)
        return o.reshape(bn2, t2, D_MODEL) @ w_out

    return (out(attend(qc, kc, vc)) if need_ctx else None), out(o_l)


def retention_chunk_scan(log_gamma, inputs, state):
    pos = jnp.arange(CHUNK, dtype=F32)
    rel = pos[:, None] - pos[None, :]
    decay = jnp.where(rel >= 0, jnp.exp(jnp.maximum(rel, 0.0) * log_gamma[:, None, None]), 0.0)
    q_decay = jnp.exp((pos + 1.0) * log_gamma[:, None]).T
    k_decay = jnp.exp((CHUNK - 1.0 - pos) * log_gamma[:, None])
    chunk_decay = jnp.exp(CHUNK * log_gamma)

    def step(s_prev, chunk):
        q, k, v = chunk
        s = jnp.einsum('blhd,bshd->bhls', q, k) * decay
        o = (jnp.einsum('bhls,bshv->blhv', s, v)
             + jnp.einsum('blhd,bhdv->blhv', q, s_prev) * q_decay[None, :, :, None])
        s_new = chunk_decay[:, None, None] * s_prev + jnp.einsum('bshd,bshv,hs->bhdv', k, v, k_decay)
        return s_new, o

    state, o = lax.scan(step, state, tuple(to_chunks(t) for t in inputs))
    return from_chunks(o), state


def retention_mixer(h_ctx, h_lat, w_in, decay_logit, norm_g, w_out, need_ctx):
    log_gamma = jax.nn.log_sigmoid(decay_logit.astype(F32))

    def prep(h, rotary):
        bn, t, _ = h.shape
        q, k, v, g = jnp.split(h @ w_in, [D_MODEL, 2 * D_MODEL, 4 * D_MODEL], axis=-1)
        q = q.reshape(bn, t, D_HEADS, D_QK) * D_QK ** -0.5
        k = k.reshape(bn, t, D_HEADS, D_QK)
        if rotary:
            q, k = rope_2d(q), rope_2d(k)
        v = v.reshape(bn, t, D_HEADS, D_V)
        return (q.astype(F32), k.astype(F32), v.astype(F32)), g

    in_c, g_c = prep(h_ctx, False)
    in_l, g_l = prep(h_lat, True)
    init = jnp.zeros((h_lat.shape[0], D_HEADS, D_QK, D_V), F32)
    o_c, o_l = bidirectional_prefix_scan(functools.partial(retention_chunk_scan, log_gamma[0]),
                                         functools.partial(retention_chunk_scan, log_gamma[1]),
                                         in_c, in_c, in_l, in_l, init)

    def out(o, g):
        return (head_norm(o, norm_g).astype(g.dtype) * jax.nn.silu(g)) @ w_out

    return (out(o_c, g_c) if need_ctx else None), out(o_l, g_l)


def peer_ffn(h, wq, sub_keys, u, v):
    bn, t, dm = h.shape
    blocks = h.reshape(bn * t // P_BLOCK, P_BLOCK, dm)

    def block(xb):
        q = (xb @ wq).reshape(P_BLOCK, P_HEADS, 2, P_DK // 2)
        s = jnp.einsum('thcd,hcnd->thcn', q, sub_keys).astype(F32)
        s_top, i_top = lax.top_k(s, P_TOPK)
        cand_s = (s_top[:, :, 0, :, None] + s_top[:, :, 1, None, :]).reshape(P_BLOCK, P_HEADS, P_TOPK * P_TOPK)
        cand_i = (i_top[:, :, 0, :, None] * P_NKEYS + i_top[:, :, 1, None, :]).reshape(P_BLOCK, P_HEADS, P_TOPK * P_TOPK)
        best_s, best_j = lax.top_k(cand_s, P_TOPK)
        idx = jnp.take_along_axis(cand_i, best_j, axis=-1)
        gate = jax.nn.softmax(best_s, axis=-1)
        act = jax.nn.gelu(jnp.einsum('td,thkd->thk', xb, u[idx]).astype(F32))
        return jnp.einsum('thk,thkd->td', (gate * act).astype(xb.dtype), v[idx])

    return lax.map(block, blocks).reshape(bn, t, dm)


def modulation(cond, w, b):
    return jnp.split(jax.nn.silu(cond) @ w + b, 6, axis=-1)


def setup_inputs(seed: int = 0) -> dict:
    key = jax.random.key(seed)
    keys = iter(jax.random.split(key, 64))

    def nrm(shape, scale):
        return jax.random.normal(next(keys), shape, F32) * scale

    def gain(shape):
        return 1.0 + nrm(shape, 0.05)

    dm = D_MODEL
    n_a, n_b, n_c, n_d = (n_mixer_layers(m) for m in range(N_MIXERS))
    a_i_bias = nrm((n_a, 2, 1, A_HEADS), 0.1)
    a_f_bias = jnp.linspace(3.0, 6.0, A_HEADS, dtype=F32) + nrm((n_a, 2, 1, A_HEADS), 0.1)
    a_b_gate = jnp.concatenate([a_i_bias, a_f_bias], axis=2).reshape(n_a, 4 * A_HEADS)
    dt0 = jnp.exp(jax.random.uniform(next(keys), (n_b, 2, B_HEADS), F32, math.log(1e-3), math.log(1e-1)))
    dt_bias = dt0 + jnp.log(-jnp.expm1(-dt0))
    a_log = jnp.log(jax.random.uniform(next(keys), (n_b, 2, B_HEADS), F32, 1.0, 16.0))
    gam = 1.0 - 2.0 ** (-5.0 - jnp.arange(D_HEADS, dtype=F32))
    ret_logit = jnp.log(gam) - jnp.log1p(-gam) + nrm((n_d, 2, D_HEADS), 0.1)
    return {
        'x': nrm((BATCH, SEQ, dm), 1.0),
        'c': nrm((BATCH, dm), 1.0),
        'ctx': nrm((BATCH, CTX_LEN, dm), 1.0),
        'c_ctx': nrm((dm,), 1.0),
        'ada_w': nrm((DEPTH, dm, 6 * dm), 0.5 * dm ** -0.5),
        'ada_b': nrm((DEPTH, 6 * dm), 0.02),
        'ln_g': gain((DEPTH, 2, dm)),
        'ln_b': nrm((DEPTH, 2, dm), 0.02),
        'peer_wq': nrm((DEPTH, dm, P_HEADS * P_DK), dm ** -0.5),
        'peer_keys': nrm((DEPTH, P_HEADS, 2, P_NKEYS, P_DK // 2), (P_DK // 2) ** -0.5),
        'peer_u': nrm((DEPTH, P_EXPERTS, dm), dm ** -0.5),
        'peer_v': nrm((DEPTH, P_EXPERTS, dm), BETA * P_HEADS ** -0.5),
        'mlstm_w_up': nrm((n_a, dm, 3 * A_DI), dm ** -0.5),
        'mlstm_conv_w': nrm((n_a, CONV_K, A_DI), CONV_K ** -0.5),
        'mlstm_conv_b': nrm((n_a, A_DI), 0.02),
        'mlstm_w_qk': nrm((n_a, A_DI, 2 * A_DI), A_DI ** -0.5),
        'mlstm_w_v': nrm((n_a, A_DI, A_DI), A_DI ** -0.5),
        'mlstm_w_gate': nrm((n_a, 3 * A_DI, 4 * A_HEADS), (3 * A_DI) ** -0.5),
        'mlstm_b_gate': a_b_gate,
        'mlstm_norm_g': gain((n_a, A_DI)),
        'mlstm_skip': gain((n_a, A_DI)),
        'mlstm_w_down': nrm((n_a, A_DI, dm), BETA * A_DI ** -0.5),
        'ssd_w_in': nrm((n_b, dm, B_DI + B_CONV_CH + 2 * B_HEADS), dm ** -0.5),
        'ssd_conv_w': nrm((n_b, CONV_K, B_CONV_CH), CONV_K ** -0.5),
        'ssd_conv_b': nrm((n_b, B_CONV_CH), 0.02),
        'ssd_dt_bias': dt_bias,
        'ssd_a_log': a_log,
        'ssd_d': gain((n_b, B_HEADS)),
        'ssd_norm_g': gain((n_b, B_DI)),
        'ssd_w_out': nrm((n_b, B_DI, dm), BETA * B_DI ** -0.5),
        'diff_w_qkv': nrm((n_c, dm, 3 * dm), dm ** -0.5),
        'diff_lambda': nrm((n_c, 4, C_HD), 0.1),
        'diff_norm_g': gain((n_c, 2 * C_HD)),
        'diff_w_out': nrm((n_c, dm, dm), BETA * dm ** -0.5),
        'ret_w_in': nrm((n_d, dm, 6 * dm), dm ** -0.5),
        'ret_decay_logit': ret_logit,
        'ret_norm_g': gain((n_d, 2 * dm)),
        'ret_w_out': nrm((n_d, 2 * dm, dm), BETA * (2 * dm) ** -0.5),
    }


def reference(x, c, ctx, c_ctx, ada_w, ada_b, ln_g, ln_b, peer_wq, peer_keys, peer_u, peer_v,
              mlstm_w_up, mlstm_conv_w, mlstm_conv_b, mlstm_w_qk, mlstm_w_v, mlstm_w_gate, mlstm_b_gate,
              mlstm_norm_g, mlstm_skip, mlstm_w_down,
              ssd_w_in, ssd_conv_w, ssd_conv_b, ssd_dt_bias, ssd_a_log, ssd_d, ssd_norm_g, ssd_w_out,
              diff_w_qkv, diff_lambda, diff_norm_g, diff_w_out,
              ret_w_in, ret_decay_logit, ret_norm_g, ret_w_out):
    h_lat, h_ctx = x, ctx
    for i in range(DEPTH):
        last = i == DEPTH - 1
        kind, j = i % N_MIXERS, i // N_MIXERS
        ml = [t[:, None, :] for t in modulation(c, ada_w[i], ada_b[i])]
        mc = modulation(c_ctx, ada_w[i], ada_b[i])
        in_l = h_lat * (1.0 + ml[1]) + ml[0]
        in_c = h_ctx * (1.0 + mc[1]) + mc[0]
        if kind == 0:
            y_c, y_l = mlstm_mixer(in_c, in_l, mlstm_w_up[j], mlstm_conv_w[j], mlstm_conv_b[j], mlstm_w_qk[j],
                                   mlstm_w_v[j], mlstm_w_gate[j], mlstm_b_gate[j], mlstm_norm_g[j],
                                   mlstm_skip[j], mlstm_w_down[j], not last)
        elif kind == 1:
            y_c, y_l = mamba2_mixer(in_c, in_l, ssd_w_in[j], ssd_conv_w[j], ssd_conv_b[j], ssd_dt_bias[j],
                                    ssd_a_log[j], ssd_d[j], ssd_norm_g[j], ssd_w_out[j], not last)
        elif kind == 2:
            y_c, y_l = diff_attention_mixer(in_c, in_l, diff_w_qkv[j], diff_lambda[j], diff_norm_g[j],
                                            diff_w_out[j], i, not last)
        else:
            y_c, y_l = retention_mixer(in_c, in_l, ret_w_in[j], ret_decay_logit[j], ret_norm_g[j],
                                       ret_w_out[j], not last)
        h_lat = layer_norm(ALPHA * h_lat + ml[2] * y_l, ln_g[i, 0], ln_b[i, 0])
        f_l = peer_ffn(h_lat * (1.0 + ml[4]) + ml[3], peer_wq[i], peer_keys[i], peer_u[i], peer_v[i])
        h_lat = layer_norm(ALPHA * h_lat + ml[5] * f_l, ln_g[i, 1], ln_b[i, 1])
        if not last:
            h_ctx = layer_norm(ALPHA * h_ctx + mc[2] * y_c, ln_g[i, 0], ln_b[i, 0])
            f_c = peer_ffn(h_ctx * (1.0 + mc[4]) + mc[3], peer_wq[i], peer_keys[i], peer_u[i], peer_v[i])
            h_ctx = layer_norm(ALPHA * h_ctx + mc[5] * f_c, ln_g[i, 1], ln_b[i, 1])
    return h_lat
```

```python
import functools
import math

import jax
import jax.numpy as jnp
from jax import lax
from jax.experimental import pallas as pl
from jax.experimental.pallas import tpu as pltpu

F32 = jnp.float32
BF = jnp.bfloat16
HI = lax.Precision.HIGHEST

D = 1024
DEPTH = 4
GRID_W = 64
CH = 128
RB = 256
LN_EPS = 1e-5
RMS_EPS = 1e-6
ALPHA = (2.0 * DEPTH) ** 0.25
ROPE_BASE = 10000.0
NEG = -jnp.inf

A_HEADS, A_HD = 4, 512
B_HEADS, B_HD, B_GROUPS, B_STATE = 32, 64, 8, 128
C_HEADS, C_HD = 8, 64
D_HEADS, D_QK, D_V = 4, 256, 512
P_HEADS, P_NK, P_TOPK = 8, 128, 16
P_TS = 768
P_TE = 256

VMEM_LIMIT = 56 * 1024 * 1024


def _cp(sem):
    return pltpu.CompilerParams(dimension_semantics=sem, vmem_limit_bytes=VMEM_LIMIT)


def _dot(a, b):
    return jnp.dot(a.astype(BF), b.astype(BF), preferred_element_type=F32)


def _dot_nt(a, b):
    return lax.dot_general(a.astype(BF), b.astype(BF), (((1,), (1,)), ((), ())),
                           preferred_element_type=F32)


def _dot_tn(a, b):
    return lax.dot_general(a.astype(BF), b.astype(BF), (((0,), (0,)), ((), ())),
                           preferred_element_type=F32)


def _dot_hi(a, b):
    return jnp.dot(a, b, preferred_element_type=F32, precision=HI)


def _dot_nt_hi(a, b):
    return lax.dot_general(a, b, (((1,), (1,)), ((), ())), preferred_element_type=F32, precision=HI)


def _col(x, idx):
    lane = lax.broadcasted_iota(jnp.int32, x.shape, 1)
    return jnp.sum(jnp.where(lane == idx, x, 0.0), axis=1, keepdims=True)


def _row(x, idx):
    sub = lax.broadcasted_iota(jnp.int32, x.shape, 0)
    return jnp.sum(jnp.where(sub == idx, x, 0.0), axis=0, keepdims=True)


def _sigmoid(x):
    return 1.0 / (1.0 + jnp.exp(-x))


def _silu(x):
    return x * _sigmoid(x)


def _softplus(x):
    return jnp.maximum(x, 0.0) + jnp.log(1.0 + jnp.exp(-jnp.abs(x)))


def _log_sigmoid(x):
    return jnp.minimum(x, 0.0) - jnp.log(1.0 + jnp.exp(-jnp.abs(x)))


def _gelu(x):
    return 0.5 * x * (1.0 + jnp.tanh(math.sqrt(2.0 / math.pi) * (x + 0.044715 * (x * x * x))))


def _layer_norm(r, g, b):
    mu = jnp.mean(r, axis=-1, keepdims=True)
    d = r - mu
    var = jnp.mean(d * d, axis=-1, keepdims=True)
    return d * lax.rsqrt(var + LN_EPS) * g + b


def _modulation(cond, ada_w, ada_b):
    R = cond.shape[0]

    def kern(c_ref, w_ref, b_ref, o_ref):
        o_ref[0] = _dot(_silu(c_ref[...]), w_ref[0]) + b_ref[0]

    return pl.pallas_call(
        kern,
        grid=(DEPTH, 6),
        in_specs=[pl.BlockSpec((R, D), lambda l, j: (0, 0)),
                  pl.BlockSpec((1, D, D), lambda l, j: (l, 0, j)),
                  pl.BlockSpec((1, 1, D), lambda l, j: (l, 0, j))],
        out_specs=pl.BlockSpec((1, R, D), lambda l, j: (l, 0, j)),
        out_shape=jax.ShapeDtypeStruct((DEPTH, R, 6 * D), F32),
        compiler_params=_cp(("parallel", "parallel")),
        name="modulation",
    )(cond, ada_w, ada_b.reshape(DEPTH, 1, 6 * D))


def _rowmm(rows, consts, w, fn, *, tn, out_dtype, mod=None, split_out=False, name):
    B, NT = rows[0][0].shape[:2]
    K, N = w.shape
    nr, nc = len(rows), len(consts)

    def kern(*refs):
        rv = [r[0] for r in refs[:nr]]
        p = nr
        mv = None
        if mod is not None:
            mod_ref = refs[p]
            mv = lambda k: mod_ref[0, 0, k:k + 1, :]
            p += 1
        cv = [r[...] for r in refs[p:p + nc]]
        w_ref, o_ref = refs[p + nc], refs[p + nc + 1]
        y = fn(rv, cv, mv, lambda x: _dot(x, w_ref[...])).astype(out_dtype)
        if split_out:
            for sp in range(N // 128):
                o_ref[sp, 0] = y[:, sp * 128:(sp + 1) * 128]
        else:
            o_ref[0] = y

    in_specs = [pl.BlockSpec((1, RB, wd), functools.partial(lambda b, i, j, cb: (b, i, cb), cb=cb))
                for (_, wd, cb) in rows]
    args = [a for (a, _, _) in rows]
    if mod is not None:
        in_specs.append(pl.BlockSpec((1, 1, 6, D), lambda b, i, j: (b, i, 0, 0)))
        args.append(mod)
    for cst in consts:
        in_specs.append(pl.BlockSpec(cst.shape, lambda b, i, j: (0, 0)))
        args.append(cst)
    in_specs.append(pl.BlockSpec((K, tn), lambda b, i, j: (0, j)))
    args.append(w)
    if split_out:
        assert tn == N
        out_specs = pl.BlockSpec((N // 128, 1, RB, 128), lambda b, i, j: (0, b, i, 0))
        out_shape = jax.ShapeDtypeStruct((N // 128, B, NT, 128), out_dtype)
    else:
        out_specs = pl.BlockSpec((1, RB, tn), lambda b, i, j: (b, i, j))
        out_shape = jax.ShapeDtypeStruct((B, NT, N), out_dtype)
    return pl.pallas_call(
        kern,
        grid=(B, NT // RB, N // tn),
        in_specs=in_specs,
        out_specs=out_specs,
        out_shape=out_shape,
        compiler_params=_cp(("parallel", "parallel", "arbitrary")),
        name=name,
    )(*args)


def _fn_inproj(rv, cv, mv, wdot, *, shift, scale):
    return wdot(rv[0] * (1.0 + mv(scale)) + mv(shift))


def _inproj(h, mod, w, *, tn, out_dtype=F32, ffn=False, split_out=False, name):
    fn = functools.partial(_fn_inproj, shift=3 if ffn else 0, scale=4 if ffn else 1)
    return _rowmm([(h, D, 0)], [], w, fn, tn=tn, out_dtype=out_dtype, mod=mod, split_out=split_out, name=name)


def _head_norm(x, nheads, hd):
    outs = []
    for hh in range(nheads):
        xh = x[:, hh * hd:(hh + 1) * hd]
        mu = jnp.mean(xh, axis=-1, keepdims=True)
        dd = xh - mu
        var = jnp.mean(dd * dd, axis=-1, keepdims=True)
        outs.append(dd * lax.rsqrt(var + LN_EPS))
    return jnp.concatenate(outs, axis=-1)


def _ln_epilogue(h, y, mv, cv):
    return _layer_norm(ALPHA * h + mv(2) * y, cv[-2], cv[-1])


def _conv_silu(src, ctile0, nct, w, b, nctx):
    B, NT, _ = src.shape
    CW = 256
    K = w.shape[0]

    def kern(x_ref, w_ref, b_ref, o_ref, scr):
        scr[0:8, :] = jnp.zeros((8, CW), F32)
        scr[8 + NT:16 + NT, :] = jnp.zeros((8, CW), F32)
        scr[8:8 + NT, :] = x_ref[0]
        t = lax.broadcasted_iota(jnp.int32, (NT, CW), 0)
        seg = t >= nctx
        acc = jnp.zeros((NT, CW), F32)
        for k in range(K):
            o = k - K // 2
            win = scr[8 + o:8 + o + NT, :]
            tt = t + o
            valid = (tt >= 0) & (tt < NT) & ((tt >= nctx) == seg)
            acc = acc + w_ref[k:k + 1, :] * jnp.where(valid, win, 0.0)
        o_ref[0] = _silu(acc + b_ref[...])

    return pl.pallas_call(
        kern,
        grid=(B, nct),
        in_specs=[pl.BlockSpec((1, NT, CW), lambda bb, c: (bb, 0, ctile0 + c)),
                  pl.BlockSpec((K, CW), lambda bb, c: (0, c)),
                  pl.BlockSpec((1, CW), lambda bb, c: (0, c))],
        out_specs=pl.BlockSpec((1, NT, CW), lambda bb, c: (bb, 0, c)),
        out_shape=jax.ShapeDtypeStruct((B, NT, nct * CW), F32),
        scratch_shapes=[pltpu.VMEM((NT + 16, CW), F32)],
        compiler_params=_cp(("parallel", "parallel")),
        name="conv_silu",
    )(src, w, b.reshape(1, -1))


def _chunk_index(d, c, ncc, nc):
    bwd = jnp.where(c < ncc, ncc - 1 - c, nc - 1 - (c - ncc))
    return jnp.where(d == 0, c, bwd)


def _tri_masks():
    r = jnp.arange(CH)
    lower = (r[None, :] <= r[:, None]).astype(F32)
    return jnp.stack([lower, lower.T])


def _mlstm_scan(qk, v, gates, tri, nctx):
    B, NT, _ = v.shape
    nc, ncc = NT // CH, nctx // CH
    HD = A_HD
    scale = HD ** -0.5

    def kern(q_ref, k_ref, v_ref, g_ref, tri_ref, o_ref, ct_ref, n_ref, m_ref):
        @pl.when(pl.program_id(2) == 0)
        def _():
            ct_ref[...] = jnp.zeros(ct_ref.shape, F32)
            n_ref[...] = jnp.zeros(n_ref.shape, F32)
            m_ref[...] = jnp.zeros(m_ref.shape, F32)

        G = g_ref[0]
        tri_m = tri_ref[0]
        mask = tri_m > 0.5
        lane = lax.broadcasted_iota(jnp.int32, (CH, 128), 1)
        Glf = jnp.where((lane >= A_HEADS) & (lane < 2 * A_HEADS), _log_sigmoid(G), 0.0)
        cum_c = _dot_hi(tri_m, Glf)
        GT = G.T
        cum_r = _dot_nt_hi(Glf.T, tri_m)
        tot = jnp.sum(Glf, axis=0, keepdims=True)
        for hh in range(A_HEADS):
            sl = slice(hh * HD, (hh + 1) * HD)
            q = q_ref[0, :, sl] * scale
            k = k_ref[0, :, sl]
            vv = v_ref[0, :, sl]
            m = jnp.max(m_ref[hh:hh + 1, :], axis=1, keepdims=True)
            ig_c = _col(G, hh)
            b_c = _col(cum_c, A_HEADS + hh)
            ig_r = _row(GT, hh)
            b_r = _row(cum_r, A_HEADS + hh)
            logw = jnp.where(mask, b_c - b_r + ig_r, NEG)
            inter = b_c + m
            m_t = jnp.maximum(inter, jnp.max(logw, axis=1, keepdims=True))
            s = _dot_nt(q, k) * jnp.exp(logw - m_t)
            a = jnp.exp(inter - m_t)
            num = _dot(s, vv) + a * _dot(q, ct_ref[hh])
            den = jnp.sum(s, axis=1, keepdims=True) + a * jnp.sum(q * n_ref[hh:hh + 1, :], axis=1, keepdims=True)
            o_ref[0, :, sl] = num / jnp.maximum(jnp.abs(den), jnp.exp(-m_t))
            b_last = _col(tot, A_HEADS + hh)
            g_r = b_last - b_r + ig_r
            g_c = b_last - b_c + ig_c
            m_new = jnp.maximum(b_last + m, jnp.max(g_r, axis=1, keepdims=True))
            decay = jnp.exp(b_last + m - m_new)
            kw = k * jnp.exp(g_c - m_new)
            ct_ref[hh] = decay * ct_ref[hh] + _dot_tn(kw, vv)
            n_ref[hh:hh + 1, :] = decay * n_ref[hh:hh + 1, :] + jnp.sum(kw, axis=0, keepdims=True)
            m_ref[hh:hh + 1, :] = jnp.broadcast_to(m_new, (1, 128))

    ci = functools.partial(_chunk_index, ncc=ncc, nc=nc)
    W = A_HEADS * HD
    return pl.pallas_call(
        kern,
        grid=(B, 2, nc),
        in_specs=[pl.BlockSpec((1, CH, W), lambda b, d, c: (b, ci(d, c), 0)),
                  pl.BlockSpec((1, CH, W), lambda b, d, c: (b, ci(d, c), 1)),
                  pl.BlockSpec((1, CH, W), lambda b, d, c: (b, ci(d, c), 0)),
                  pl.BlockSpec((1, CH, 128), lambda b, d, c: (b, ci(d, c), d)),
                  pl.BlockSpec((1, CH, CH), lambda b, d, c: (d, 0, 0))],
        out_specs=pl.BlockSpec((1, CH, W), lambda b, d, c: (b, ci(d, c), d)),
        out_shape=jax.ShapeDtypeStruct((B, NT, 2 * W), F32),
        scratch_shapes=[pltpu.VMEM((A_HEADS, HD, HD), F32), pltpu.VMEM((8, HD), F32), pltpu.VMEM((8, 128), F32)],
        compiler_params=_cp(("parallel", "arbitrary", "arbitrary")),
        name="mlstm_scan",
    )(qk, qk, v, gates, tri)


def _rope_blocks(x, cos, sin, half):
    per = cos.shape[1] // 128
    lane = lax.broadcasted_iota(jnp.int32, (x.shape[0], 128), 1)
    outs = []
    for c in range(x.shape[1] // 128):
        xb = x[:, c * 128:(c + 1) * 128]
        cb = cos[:, (c % per) * 128:(c % per + 1) * 128]
        sb = sin[:, (c % per) * 128:(c % per + 1) * 128]
        if half == 64:
            partner = pltpu.roll(xb, 64, 1)
        else:
            lo = (lane % (2 * half)) < half
            partner = jnp.where(lo, pltpu.roll(xb, 128 - half, 1), pltpu.roll(xb, half, 1))
        outs.append(xb * cb + partner * sb)
    return jnp.concatenate(outs, axis=-1)


def _retention_scan(proj, cos, sin, dec, qd, kd, cd, nctx):
    B, NT, _ = proj.shape
    nc, ncc = NT // CH, nctx // CH
    scale = D_QK ** -0.5

    def kern(q_ref, k_ref, v_ref, cos_ref, sin_ref, dec_ref, qd_ref, kd_ref, cd_ref, o_ref, s_ref):
        @pl.when(pl.program_id(2) == 0)
        def _():
            s_ref[...] = jnp.zeros(s_ref.shape, F32)

        cos_t, sin_t = cos_ref[...], sin_ref[...]
        qd_t, kd_t = qd_ref[0], kd_ref[0]
        cd_t = cd_ref[0, 0:1, :]
        for hh in range(D_HEADS):
            qh = _rope_blocks(q_ref[0, :, hh * D_QK:(hh + 1) * D_QK], cos_t, sin_t, 64) * scale
            kh = _rope_blocks(k_ref[0, :, hh * D_QK:(hh + 1) * D_QK], cos_t, sin_t, 64)
            vh = v_ref[0, :, hh * D_V:(hh + 1) * D_V]
            s = _dot_nt(qh, kh) * dec_ref[0, hh]
            o_ref[0, :, hh * D_V:(hh + 1) * D_V] = _dot(s, vh) + _dot(qh, s_ref[hh]) * _col(qd_t, hh)
            s_ref[hh] = _col(cd_t, hh) * s_ref[hh] + _dot_tn(kh * _col(kd_t, hh), vh)

    ci = functools.partial(_chunk_index, ncc=ncc, nc=nc)
    WV = D_HEADS * D_V
    return pl.pallas_call(
        kern,
        grid=(B, 2, nc),
        in_specs=[pl.BlockSpec((1, CH, D), lambda b, d, c: (b, ci(d, c), 0)),
                  pl.BlockSpec((1, CH, D), lambda b, d, c: (b, ci(d, c), 1)),
                  pl.BlockSpec((1, CH, WV), lambda b, d, c: (b, ci(d, c), 1)),
                  pl.BlockSpec((CH, 256), lambda b, d, c: (ci(d, c), 0)),
                  pl.BlockSpec((CH, 256), lambda b, d, c: (ci(d, c), 0)),
                  pl.BlockSpec((1, D_HEADS, CH, CH), lambda b, d, c: (d, 0, 0, 0)),
                  pl.BlockSpec((1, CH, 128), lambda b, d, c: (d, 0, 0)),
                  pl.BlockSpec((1, CH, 128), lambda b, d, c: (d, 0, 0)),
                  pl.BlockSpec((1, 8, 128), lambda b, d, c: (d, 0, 0))],
        out_specs=pl.BlockSpec((1, CH, WV), lambda b, d, c: (b, ci(d, c), d)),
        out_shape=jax.ShapeDtypeStruct((B, NT, 2 * WV), F32),
        scratch_shapes=[pltpu.VMEM((D_HEADS, D_QK, D_V), F32)],
        compiler_params=_cp(("parallel", "arbitrary", "arbitrary")),
        name="retention_scan",
    )(proj, proj, proj, cos, sin, dec, qd, kd, cd)


def _ssd_scan(xbc, dtraw, dt_bias, a_rate, expand, tri, nctx):
    B, NT, _ = xbc.shape
    nc, ncc = NT // CH, nctx // CH
    R = B_HEADS // B_GROUPS
    GW = R * B_HD
    DI = B_HEADS * B_HD

    def kern(x_ref, b_ref, c_ref, dt_ref, bias_ref, a_ref, e_ref, tri_ref, o_ref, st_ref):
        @pl.when(pl.program_id(2) == 0)
        def _():
            st_ref[...] = jnp.zeros(st_ref.shape, F32)

        tri_m = tri_ref[0]
        mask = tri_m > 0.5
        E = e_ref[...]
        dt = _softplus(dt_ref[0] + bias_ref[0])
        dta = dt * a_ref[0]
        cum_c = _dot_hi(tri_m, dta)
        cum_r = _dot_nt_hi(dta.T, tri_m)
        a_last = jnp.sum(dta, axis=0, keepdims=True)
        xdt = x_ref[0] * _dot_hi(dt, E)
        eacs = _dot_hi(jnp.exp(cum_c), E)
        wst = _dot_hi(jnp.exp(a_last - cum_c), E)
        da = _dot_hi(jnp.broadcast_to(jnp.exp(a_last), (8, 128)), E)
        lane = lax.broadcasted_iota(jnp.int32, (CH, 128), 1)
        lo = lane < B_HD
        for g in range(B_GROUPS):
            bg = b_ref[0, :, g * B_STATE:(g + 1) * B_STATE]
            cg = c_ref[0, :, g * B_STATE:(g + 1) * B_STATE]
            cb = _dot_nt(cg, bg)
            xg = xdt[:, g * GW:(g + 1) * GW]
            y2 = _dot(cg, st_ref[g]) * eacs[:, g * GW:(g + 1) * GW]
            for pr in range(R // 2):
                xp = xg[:, pr * 128:(pr + 1) * 128]
                acc = y2[:, pr * 128:(pr + 1) * 128]
                for u in range(2):
                    hd = g * R + pr * 2 + u
                    seg = jnp.exp(jnp.where(mask, _col(cum_c, hd) - _row(cum_r, hd), NEG))
                    xm = jnp.where(lo if u == 0 else jnp.logical_not(lo), xp, 0.0)
                    acc = acc + _dot(cb * seg, xm)
                o_ref[0, :, g * GW + pr * 128:g * GW + (pr + 1) * 128] = acc
            st_ref[g] = da[0:1, g * GW:(g + 1) * GW] * st_ref[g] + _dot_tn(bg, xg * wst[:, g * GW:(g + 1) * GW])

    ci = functools.partial(_chunk_index, ncc=ncc, nc=nc)
    GS = B_GROUPS * B_STATE
    return pl.pallas_call(
        kern,
        grid=(B, 2, nc),
        in_specs=[pl.BlockSpec((1, CH, DI), lambda b, d, c: (b, ci(d, c), 0)),
                  pl.BlockSpec((1, CH, GS), lambda b, d, c: (b, ci(d, c), 2)),
                  pl.BlockSpec((1, CH, GS), lambda b, d, c: (b, ci(d, c), 3)),
                  pl.BlockSpec((1, CH, 128), lambda b, d, c: (b, ci(d, c), d)),
                  pl.BlockSpec((1, 1, 128), lambda b, d, c: (d, 0, 0)),
                  pl.BlockSpec((1, 1, 128), lambda b, d, c: (d, 0, 0)),
                  pl.BlockSpec((128, DI), lambda b, d, c: (0, 0)),
                  pl.BlockSpec((1, CH, CH), lambda b, d, c: (d, 0, 0))],
        out_specs=pl.BlockSpec((1, CH, DI), lambda b, d, c: (b, ci(d, c), d)),
        out_shape=jax.ShapeDtypeStruct((B, NT, 2 * DI), F32),
        scratch_shapes=[pltpu.VMEM((B_GROUPS, B_STATE, GW), F32)],
        compiler_params=_cp(("parallel", "arbitrary", "arbitrary")),
        name="ssd_scan",
    )(xbc, xbc, xbc, dtraw, dt_bias, a_rate, expand, tri)


def _attn_prep(qkv, cos, sin):
    B, NT, _ = qkv.shape
    scale = C_HD ** -0.5

    def kern(x_ref, cos_ref, sin_ref, o_ref):
        j = pl.program_id(2)
        x = x_ref[0]

        @pl.when(j < 2)
        def _():
            y = _rope_blocks(x, cos_ref[...], sin_ref[...], C_HD // 4)
            o_ref[0] = (y * jnp.where(j == 0, scale, 1.0)).astype(BF)

        @pl.when(j == 2)
        def _():
            o_ref[0] = x.astype(BF)

    return pl.pallas_call(
        kern,
        grid=(B, NT // RB, 3),
        in_specs=[pl.BlockSpec((1, RB, D), lambda b, i, j: (b, i, j)),
                  pl.BlockSpec((RB, 128), lambda b, i, j: (i, 0)),
                  pl.BlockSpec((RB, 128), lambda b, i, j: (i, 0))],
        out_specs=pl.BlockSpec((1, RB, D), lambda b, i, j: (b, i, j)),
        out_shape=jax.ShapeDtypeStruct((B, NT, 3 * D), BF),
        compiler_params=_cp(("parallel", "parallel", "arbitrary")),
        name="attn_prep",
    )(qkv, cos, sin)


def _diff_attention(qkvb, lam, norm_g, out_scale, nctx):
    B, NT, _ = qkvb.shape
    TQ = RB
    HP = 2 * C_HD

    def kern(q_ref, k_ref, v_ref, lam_ref, g_ref, o_ref):
        lam_v = lam_ref[0:1, 0:1]
        lane = lax.broadcasted_iota(jnp.int32, (TQ, HP), 1)
        lo = lane < C_HD

        def attend(nk):
            for hp in range(C_HEADS):
                qp = q_ref[0, :, hp * HP:(hp + 1) * HP]
                kp = k_ref[0, 0:nk, hp * HP:(hp + 1) * HP]
                vp = v_ref[0, 0:nk, hp * HP:(hp + 1) * HP]
                outs = []
                for u in range(2):
                    qm = jnp.where(lo if u == 0 else jnp.logical_not(lo), qp, jnp.zeros_like(qp))
                    s = lax.dot_general(qm, kp, (((1,), (1,)), ((), ())), preferred_element_type=F32)
                    p = jnp.exp(s - jnp.max(s, axis=1, keepdims=True))
                    z = jnp.sum(p, axis=1, keepdims=True)
                    outs.append(jnp.dot(p.astype(BF), vp, preferred_element_type=F32) / z)
                o = outs[0] - lam_v * outs[1]
                o = o * lax.rsqrt(jnp.mean(o * o, axis=-1, keepdims=True) + RMS_EPS)
                o_ref[0, :, hp * HP:(hp + 1) * HP] = o * g_ref[...] * out_scale

        is_ctx = pl.program_id(1) * TQ < nctx

        @pl.when(is_ctx)
        def _():
            attend(nctx)

        @pl.when(jnp.logical_not(is_ctx))
        def _():
            attend(NT)

    return pl.pallas_call(
        kern,
        grid=(B, NT // TQ),
        in_specs=[pl.BlockSpec((1, TQ, D), lambda b, i: (b, i, 0)),
                  pl.BlockSpec((1, NT, D), lambda b, i: (b, 0, 1)),
                  pl.BlockSpec((1, NT, D), lambda b, i: (b, 0, 2)),
                  pl.BlockSpec((8, 128), lambda b, i: (0, 0)),
                  pl.BlockSpec((1, HP), lambda b, i: (0, 0))],
        out_specs=pl.BlockSpec((1, TQ, D), lambda b, i: (b, i, 0)),
        out_shape=jax.ShapeDtypeStruct((B, NT, D), F32),
        compiler_params=_cp(("parallel", "arbitrary")),
        name="diff_attention",
    )(qkvb, qkvb, qkvb, lam, norm_g.reshape(1, HP))


def _peer_ln(h, q16, mod, keys, u, vt, ln_g, ln_b):
    B, NT, _ = h.shape
    TS, TE = P_TS, P_TE
    E = u.shape[0]
    ne = E // TE
    NTB = TS // 128
    NIB = TE // P_NK
    RPB = TS // RB
    hf = h.reshape(B * NT, D)
    modf = mod.reshape(B * (NT // RB), 6, D)
    pairs = [(ra, rb) for ra in range(P_TOPK) for rb in range(P_TOPK // (ra + 1))]
    NCAND = 8 * ((len(pairs) + 7) // 8)

    def kern(h_ref, q_ref, mod_ref, keys_ref, u_ref, vt_ref, g_ref, b_ref, o_ref,
             xin, s1c, f1, s2c, e2m, thr, cand, a_sc, w_sc, acc):
        e = pl.program_id(1)

        @pl.when(e == 0)
        def _():
            for r in range(RPB):
                hx = h_ref[r * RB:(r + 1) * RB, :]
                xin[r * RB:(r + 1) * RB, :] = (hx * (1.0 + mod_ref[r, 4:5, :]) + mod_ref[r, 3:4, :]).astype(BF)
            acc[...] = jnp.zeros(acc.shape, F32)

            def head_body(hh, carry):
                for tb in range(NTB):
                    tsl = slice(tb * 128, (tb + 1) * 128)
                    vals = []
                    for c in range(2):
                        s = lax.dot_general(keys_ref[2 * hh + c], q_ref[2 * hh + c, tsl, :],
                                            (((1,), (1,)), ((), ())), preferred_element_type=F32)
                        work = s
                        tops = []
                        for r in range(P_TOPK):
                            mx = jnp.max(work, axis=0, keepdims=True)
                            tops.append(mx)
                            if r + 1 < P_TOPK:
                                work = jnp.where(work == mx, NEG, work)
                        vals.append((jnp.where(s >= tops[-1], s, NEG), tops))
                    (s1, ta), (s2, tbv) = vals
                    cand[...] = jnp.full(cand.shape, NEG, F32)
                    for n, (ra, rb) in enumerate(pairs):
                        cand[n:n + 1, :] = ta[ra] + tbv[rb]
                    cv = cand[...]
                    work = cv
                    for r in range(P_TOPK):
                        th = jnp.max(work, axis=0, keepdims=True)
                        if r + 1 < P_TOPK:
                            work = jnp.where(work == th, NEG, work)
                    top = ta[0] + tbv[0]
                    z = jnp.sum(jnp.where(cv >= th, jnp.exp(cv - top), 0.0), axis=0, keepdims=True)
                    s1c[hh, :, tsl] = s1
                    f1[hh, :, tsl] = jnp.exp(s1 - ta[0])
                    s2c[hh, :, tsl] = s2
                    e2m[hh, :, tsl] = jnp.exp(s2 - tbv[0]) / z
                    thr[hh, :, tsl] = th
                return carry

            lax.fori_loop(0, P_HEADS, head_body, 0)

        a_sc[...] = lax.dot_general(u_ref[...], xin[...], (((1,), (1,)), ((), ())), preferred_element_type=F32)
        for ib in range(NIB):
            i = e * NIB + ib
            i8 = pl.multiple_of((i // 8) * 8, 8)
            for tb in range(NTB):
                tsl = slice(tb * 128, (tb + 1) * 128)
                gm = jnp.zeros((P_NK, 128), F32)
                for hh in range(P_HEADS):
                    s1row = _row(s1c[hh, pl.ds(i8, 8), tsl], i % 8)
                    frow = _row(f1[hh, pl.ds(i8, 8), tsl], i % 8)
                    sel = (s2c[hh, :, tsl] + s1row) >= thr[hh, :, tsl]
                    gm = gm + jnp.where(sel, e2m[hh, :, tsl], 0.0) * frow
                w_sc[ib * P_NK:(ib + 1) * P_NK, tsl] = (gm * _gelu(a_sc[ib * P_NK:(ib + 1) * P_NK, tsl])).astype(BF)
        acc[...] += jnp.dot(vt_ref[...], w_sc[...], preferred_element_type=F32)

        @pl.when(e == ne - 1)
        def _():
            y = acc[...].T
            for r in range(RPB):
                hx = h_ref[r * RB:(r + 1) * RB, :]
                o_ref[r * RB:(r + 1) * RB, :] = _layer_norm(
                    ALPHA * hx + mod_ref[r, 5:6, :] * y[r * RB:(r + 1) * RB, :], g_ref[...], b_ref[...])

    out = pl.pallas_call(
        kern,
        grid=(B * NT // TS, ne),
        in_specs=[pl.BlockSpec((TS, D), lambda s, e: (s, 0)),
                  pl.BlockSpec((2 * P_HEADS, TS, P_NK), lambda s, e: (0, s, 0)),
                  pl.BlockSpec((RPB, 6, D), lambda s, e: (s, 0, 0)),
                  pl.BlockSpec((2 * P_HEADS, P_NK, P_NK), lambda s, e: (0, 0, 0)),
                  pl.BlockSpec((TE, D), lambda s, e: (e, 0)),
                  pl.BlockSpec((D, TE), lambda s, e: (0, e)),
                  pl.BlockSpec((1, D), lambda s, e: (0, 0)),
                  pl.BlockSpec((1, D), lambda s, e: (0, 0))],
        out_specs=pl.BlockSpec((TS, D), lambda s, e: (s, 0)),
        out_shape=jax.ShapeDtypeStruct((B * NT, D), F32),
        scratch_shapes=[pltpu.VMEM((TS, D), BF),
                        pltpu.VMEM((P_HEADS, P_NK, TS), F32), pltpu.VMEM((P_HEADS, P_NK, TS), F32),
                        pltpu.VMEM((P_HEADS, P_NK, TS), F32), pltpu.VMEM((P_HEADS, P_NK, TS), F32),
                        pltpu.VMEM((P_HEADS, 1, TS), F32),
                        pltpu.VMEM((NCAND, 128), F32),
                        pltpu.VMEM((TE, TS), F32), pltpu.VMEM((TE, TS), BF),
                        pltpu.VMEM((D, TS), F32)],
        compiler_params=_cp(("parallel", "arbitrary")),
        name="peer",
    )(hf, q16, modf, keys, u, vt, ln_g.reshape(1, D), ln_b.reshape(1, D))
    return out.reshape(B, NT, D)


def _rope_table(nctx, nlat, half, reps):
    t = jnp.arange(nlat, dtype=jnp.int32)
    freqs = ROPE_BASE ** (-jnp.arange(half, dtype=F32) / half)
    sign = jnp.concatenate([-jnp.ones((half,), F32), jnp.ones((half,), F32)])
    cs, sn = [], []
    for pos in (t // GRID_W, t % GRID_W):
        ang = pos.astype(F32)[:, None] * freqs
        cs.append(jnp.concatenate([jnp.cos(ang), jnp.cos(ang)], axis=-1))
        sn.append(jnp.concatenate([jnp.sin(ang), jnp.sin(ang)], axis=-1) * sign)
    cos = jnp.tile(jnp.concatenate(cs, axis=-1), (1, reps))
    sin = jnp.tile(jnp.concatenate(sn, axis=-1), (1, reps))
    cos = jnp.concatenate([jnp.ones((nctx, cos.shape[1]), F32), cos], axis=0)
    sin = jnp.concatenate([jnp.zeros((nctx, sin.shape[1]), F32), sin], axis=0)
    return cos, sin


def _retention_tables(decay_logit):
    lg = jax.nn.log_sigmoid(decay_logit.astype(F32))
    pos = jnp.arange(CH, dtype=F32)
    rel = pos[:, None] - pos[None, :]
    lower = jnp.where(rel >= 0, jnp.exp(jnp.maximum(rel, 0.0) * lg[:, :, None, None]), 0.0)
    dec = jnp.stack([lower[0], jnp.swapaxes(lower[1], -1, -2)])
    qd_f = jnp.exp((pos + 1.0)[:, None] * lg[0][None, :])
    kd_f = jnp.exp((CH - 1.0 - pos)[:, None] * lg[0][None, :])
    qd_b = jnp.exp((CH - pos)[:, None] * lg[1][None, :])
    kd_b = jnp.exp(pos[:, None] * lg[1][None, :])
    pad = lambda a: jnp.pad(a, ((0, 0), (0, 128 - a.shape[1])))
    qd = jnp.stack([pad(qd_f), pad(qd_b)])
    kd = jnp.stack([pad(kd_f), pad(kd_b)])
    cd = jnp.broadcast_to(jnp.pad(jnp.exp(CH * lg), ((0, 0), (0, 128 - D_HEADS)))[:, None, :], (2, 8, 128))
    return dec, qd, kd, cd


def _mlstm_layer(h, mod, lnc, tri, nctx, w_up, conv_w, conv_b, w_qk, w_v, w_gate, b_gate, norm_g, skip, w_down):
    DI = A_HEADS * A_HD
    up = _inproj(h, mod, w_up.astype(BF), tn=1024, name="mlstm_up")
    xc = _conv_silu(up, 0, DI // 256, conv_w, conv_b, nctx)
    plain = lambda rv, cv, mv, wdot: wdot(rv[0])
    qk = _rowmm([(xc, DI, 0)], [], w_qk.astype(BF), plain, tn=1024, out_dtype=F32, name="mlstm_qk")
    v = _rowmm([(up, DI, 0)], [], w_v.astype(BF), plain, tn=1024, out_dtype=F32, name="mlstm_v")
    wg = w_gate.reshape(3 * DI, 4, A_HEADS)
    bg = b_gate.reshape(4, A_HEADS)
    zw = jnp.zeros((3 * DI, 128 - 2 * A_HEADS), F32)
    zb = jnp.zeros((128 - 2 * A_HEADS,), F32)
    wg2 = jnp.concatenate([wg[:, 0], wg[:, 1], zw, wg[:, 2], wg[:, 3], zw], axis=1).astype(BF)
    bg2 = jnp.concatenate([bg[0], bg[1], zb, bg[2], bg[3], zb]).reshape(1, 256)

    def gate_fn(rv, cv, mv, wdot):
        return _dot(rv[0], cv[0]) + _dot(rv[1], cv[1]) + wdot(rv[2]) + cv[2]

    gates = _rowmm([(qk, DI, 0), (qk, DI, 1), (v, DI, 0)], [wg2[:DI], wg2[DI:2 * DI], bg2], wg2[2 * DI:], gate_fn,
                   tn=256, out_dtype=F32, name="mlstm_gates")
    hs = _mlstm_scan(qk, v, gates, tri, nctx)

    def out_fn(rv, cv, mv, wdot):
        hf, hb, z, o_pre, xcv, hres = rv
        hh = _sigmoid(o_pre) * (hf + hb)
        y = _head_norm(hh, A_HEADS, A_HD) * cv[0] + cv[1] * xcv
        return _ln_epilogue(hres, wdot(y * _silu(z)), mv, cv)

    return _rowmm([(hs, DI, 0), (hs, DI, 1), (up, DI, 1), (up, DI, 2), (xc, DI, 0), (h, D, 0)],
                  [norm_g.reshape(1, DI), skip.reshape(1, DI)] + lnc, w_down.astype(BF), out_fn,
                  tn=D, out_dtype=F32, mod=mod, name="mlstm_out")


def _ssd_layer(h, mod, lnc, tri, nctx, w_in, conv_w, conv_b, dt_bias, a_log, d_skip, norm_g, w_out):
    DI = B_HEADS * B_HD
    CC = DI + 2 * B_GROUPS * B_STATE
    proj = _inproj(h, mod, w_in[:, :DI + CC].astype(BF), tn=1024, name="ssd_in")
    wdt = w_in[:, DI + CC:].reshape(D, 2, B_HEADS)
    zpad = jnp.zeros((D, 128 - B_HEADS), F32)
    wdt2 = jnp.concatenate([wdt[:, 0], zpad, wdt[:, 1], zpad], axis=1).astype(BF)
    dtraw = _inproj(h, mod, wdt2, tn=256, name="ssd_dt")
    xbc = _conv_silu(proj, DI // 256, CC // 256, conv_w, conv_b, nctx)
    padl = lambda a: jnp.pad(a.astype(F32), ((0, 0), (0, 128 - B_HEADS))).reshape(2, 1, 128)
    expand = (jnp.arange(128)[:, None] == (jnp.arange(DI)[None, :] // B_HD)).astype(F32)
    ys = _ssd_scan(xbc, dtraw, padl(dt_bias), padl(-jnp.exp(a_log.astype(F32))), expand, tri, nctx)
    dvec = jnp.repeat(d_skip.astype(F32), B_HD).reshape(1, DI)
    GW = DI // B_GROUPS

    def out_fn(rv, cv, mv, wdot):
        yf, yb, xs, z, hres = rv
        y = (yf + yb + cv[0] * xs) * _silu(z)
        outs = []
        for g in range(B_GROUPS):
            yg = y[:, g * GW:(g + 1) * GW]
            outs.append(yg * lax.rsqrt(jnp.mean(yg * yg, axis=-1, keepdims=True) + RMS_EPS))
        return _ln_epilogue(hres, wdot(jnp.concatenate(outs, axis=-1) * cv[1]), mv, cv)

    return _rowmm([(ys, DI, 0), (ys, DI, 1), (xbc, DI, 0), (proj, DI, 0), (h, D, 0)],
                  [dvec, norm_g.reshape(1, DI)] + lnc, w_out.astype(BF), out_fn,
                  tn=D, out_dtype=F32, mod=mod, name="ssd_out")


def _diff_layer(h, mod, lnc, nctx, layer_idx, w_qkv, lam_vecs, norm_g, w_out):
    B, NT, _ = h.shape
    lam_init = 0.8 - 0.6 * math.exp(-0.3 * layer_idx)
    lv = lam_vecs.astype(F32)
    lam = jnp.exp(jnp.sum(lv[0] * lv[1])) - jnp.exp(jnp.sum(lv[2] * lv[3])) + lam_init
    qkv = _inproj(h, mod, w_qkv.astype(BF), tn=1024, name="diff_qkv")
    cos, sin = _rope_table(nctx, NT - nctx, C_HD // 4, 2)
    qkvb = _attn_prep(qkv, cos, sin)
    o = _diff_attention(qkvb, jnp.full((8, 128), lam, F32), norm_g, 1.0 - lam_init, nctx)

    def out_fn(rv, cv, mv, wdot):
        return _ln_epilogue(rv[1], wdot(rv[0]), mv, cv)

    return _rowmm([(o, D, 0), (h, D, 0)], lnc, w_out.astype(BF), out_fn, tn=D, out_dtype=F32, mod=mod,
                  name="diff_out")


def _retention_layer(h, mod, lnc, nctx, w_in, decay_logit, norm_g, w_out):
    B, NT, _ = h.shape
    DV = D_HEADS * D_V
    proj = _inproj(h, mod, w_in.astype(BF), tn=1024, name="ret_in")
    cos, sin = _rope_table(nctx, NT - nctx, D_QK // 4, 1)
    dec, qd, kd, cd = _retention_tables(decay_logit)
    os_ = _retention_scan(proj, cos, sin, dec, qd, kd, cd, nctx)

    def out_fn(rv, cv, mv, wdot):
        of, ob, g, hres = rv
        y = _head_norm(of + ob, D_HEADS, D_V) * cv[0] * _silu(g)
        return _ln_epilogue(hres, wdot(y), mv, cv)

    return _rowmm([(os_, DV, 0), (os_, DV, 1), (proj, DV, 2), (h, D, 0)], [norm_g.reshape(1, DV)] + lnc,
                  w_out.astype(BF), out_fn, tn=D, out_dtype=F32, mod=mod, name="ret_out")


def kernel(x, c, ctx, c_ctx, ada_w, ada_b, ln_g, ln_b, peer_wq, peer_keys, peer_u, peer_v, mlstm_w_up, mlstm_conv_w, mlstm_conv_b, mlstm_w_qk, mlstm_w_v, mlstm_w_gate, mlstm_b_gate, mlstm_norm_g, mlstm_skip, mlstm_w_down, ssd_w_in, ssd_conv_w, ssd_conv_b, ssd_dt_bias, ssd_a_log, ssd_d, ssd_norm_g, ssd_w_out, diff_w_qkv, diff_lambda, diff_norm_g, diff_w_out, ret_w_in, ret_decay_logit, ret_norm_g, ret_w_out):
    B, nlat, _ = x.shape
    nctx = ctx.shape[1]
    NT = nctx + nlat
    nrb = NT // RB
    h = jnp.concatenate([ctx, x], axis=1)
    rows = 8 * ((B + 1 + 7) // 8)
    cond = jnp.concatenate([c, c_ctx[None, :], jnp.zeros((rows - B - 1, D), F32)], axis=0)
    modall = _modulation(cond, ada_w, ada_b)
    tri = _tri_masks()
    for i in range(DEPTH):
        kind, j = i % 4, i // 4
        ml = modall[i, :B].reshape(B, 1, 6, D)
        mc = modall[i, B].reshape(1, 1, 6, D)
        mod = jnp.concatenate([jnp.broadcast_to(mc, (B, nctx // RB, 6, D)),
                               jnp.broadcast_to(ml, (B, nlat // RB, 6, D))], axis=1).reshape(B, nrb, 6, D)
        lnc = [ln_g[i, 0].reshape(1, D), ln_b[i, 0].reshape(1, D)]
        if kind == 0:
            h = _mlstm_layer(h, mod, lnc, tri, nctx, mlstm_w_up[j], mlstm_conv_w[j], mlstm_conv_b[j], mlstm_w_qk[j],
                             mlstm_w_v[j], mlstm_w_gate[j], mlstm_b_gate[j], mlstm_norm_g[j], mlstm_skip[j],
                             mlstm_w_down[j])
        elif kind == 1:
            h = _ssd_layer(h, mod, lnc, tri, nctx, ssd_w_in[j], ssd_conv_w[j], ssd_conv_b[j], ssd_dt_bias[j],
                           ssd_a_log[j], ssd_d[j], ssd_norm_g[j], ssd_w_out[j])
        elif kind == 2:
            h = _diff_layer(h, mod, lnc, nctx, i, diff_w_qkv[j], diff_lambda[j], diff_norm_g[j], diff_w_out[j])
        else:
            h = _retention_layer(h, mod, lnc, nctx, ret_w_in[j], ret_decay_logit[j], ret_norm_g[j], ret_w_out[j])
        q16 = _inproj(h, mod, peer_wq[i].astype(BF), tn=2 * P_HEADS * P_NK, out_dtype=BF, ffn=True, split_out=True,
                      name="peer_q").reshape(2 * P_HEADS, B * NT, P_NK)
        keys = peer_keys[i].reshape(2 * P_HEADS, P_NK, P_NK).astype(BF)
        h = _peer_ln(h, q16, mod, keys, peer_u[i].astype(BF), peer_v[i].astype(BF).T, ln_g[i, 1], ln_b[i, 1])
    return h[:, nctx:, :]
```

```python
import functools
import math

import jax
import jax.numpy as jnp
from jax import lax
from jax.experimental import pallas as pl
from jax.experimental.pallas import tpu as pltpu

F32 = jnp.float32
BF = jnp.bfloat16
HI = lax.Precision.HIGHEST

D = 1024
DEPTH = 4
GRID_W = 64
CH = 128
RB = 256
LN_EPS = 1e-5
RMS_EPS = 1e-6
ALPHA = (2.0 * DEPTH) ** 0.25
ROPE_BASE = 10000.0
NEG = -jnp.inf

A_HEADS, A_HD = 4, 512
B_HEADS, B_HD, B_GROUPS, B_STATE = 32, 64, 8, 128
C_HEADS, C_HD = 8, 64
D_HEADS, D_QK, D_V = 4, 256, 512
P_HEADS, P_NK, P_TOPK = 8, 128, 16
P_TS = 768
P_TE = 1024

VMEM_LIMIT = 56 * 1024 * 1024


def _cp(sem):
    return pltpu.CompilerParams(dimension_semantics=sem, vmem_limit_bytes=VMEM_LIMIT)


def _dot(a, b):
    return jnp.dot(a.astype(BF), b.astype(BF), preferred_element_type=F32)


def _dot_nt(a, b):
    return lax.dot_general(a.astype(BF), b.astype(BF), (((1,), (1,)), ((), ())),
                           preferred_element_type=F32)


def _dot_tn(a, b):
    return lax.dot_general(a.astype(BF), b.astype(BF), (((0,), (0,)), ((), ())),
                           preferred_element_type=F32)


def _dot_hi(a, b):
    return jnp.dot(a, b, preferred_element_type=F32, precision=HI)


def _dot_nt_hi(a, b):
    return lax.dot_general(a, b, (((1,), (1,)), ((), ())), preferred_element_type=F32, precision=HI)


def _expand(x, e01):
    hi = x.astype(BF)
    lo = (x - hi.astype(F32)).astype(BF)
    return (jnp.dot(hi, e01.astype(BF), preferred_element_type=F32)
            + jnp.dot(lo, e01.astype(BF), preferred_element_type=F32))


def _col(x, idx):
    lane = lax.broadcasted_iota(jnp.int32, x.shape, 1)
    return jnp.sum(jnp.where(lane == idx, x, 0.0), axis=1, keepdims=True)


def _row(x, idx):
    sub = lax.broadcasted_iota(jnp.int32, x.shape, 0)
    return jnp.sum(jnp.where(sub == idx, x, 0.0), axis=0, keepdims=True)


def _sigmoid(x):
    return 1.0 / (1.0 + jnp.exp(-x))


def _silu(x):
    return x * _sigmoid(x)


def _softplus(x):
    return jnp.maximum(x, 0.0) + jnp.log(1.0 + jnp.exp(-jnp.abs(x)))


def _log_sigmoid(x):
    return jnp.minimum(x, 0.0) - jnp.log(1.0 + jnp.exp(-jnp.abs(x)))


def _gelu(x):
    return 0.5 * x * (1.0 + jnp.tanh(math.sqrt(2.0 / math.pi) * (x + 0.044715 * (x * x * x))))


def _layer_norm(r, g, b):
    mu = jnp.mean(r, axis=-1, keepdims=True)
    d = r - mu
    var = jnp.mean(d * d, axis=-1, keepdims=True)
    return d * lax.rsqrt(var + LN_EPS) * g + b


def _modulation(cond, ada_w, ada_b):
    R = cond.shape[0]

    def kern(c_ref, w_ref, b_ref, o_ref):
        o_ref[0] = _dot(_silu(c_ref[...]), w_ref[0]) + b_ref[0]

    return pl.pallas_call(
        kern,
        grid=(DEPTH, 6),
        in_specs=[pl.BlockSpec((R, D), lambda l, j: (0, 0)),
                  pl.BlockSpec((1, D, D), lambda l, j: (l, 0, j)),
                  pl.BlockSpec((1, 1, D), lambda l, j: (l, 0, j))],
        out_specs=pl.BlockSpec((1, R, D), lambda l, j: (l, 0, j)),
        out_shape=jax.ShapeDtypeStruct((DEPTH, R, 6 * D), F32),
        compiler_params=_cp(("parallel", "parallel")),
        name="modulation",
    )(cond, ada_w, ada_b.reshape(DEPTH, 1, 6 * D))


def _rowmm(rows, consts, w, pro, epi=None, *, out_dtype, mod=None, split_out=False, name):
    B, NT = rows[0][0].shape[:2]
    K, N = w.shape
    tn = min(N, 1024)
    assert N % tn == 0 and (epi is None or N == tn)
    nr, nc = len(rows), len(consts)

    def kern(*refs):
        rv = [r[0] for r in refs[:nr]]
        p = nr
        mv = None
        if mod is not None:
            mod_ref = refs[p]
            mv = lambda k: mod_ref[0, 0, k:k + 1, :]
            p += 1
        cv = [r[...] for r in refs[p:p + nc]]
        w_ref, o_ref = refs[p + nc], refs[p + nc + 1]
        xb = pro(rv, cv, mv).astype(BF)
        for t in range(N // tn):
            y = jnp.dot(xb, w_ref[:, t * tn:(t + 1) * tn], preferred_element_type=F32)
            if epi is not None:
                y = epi(y, rv, cv, mv)
            y = y.astype(out_dtype)
            if split_out:
                for sp in range(tn // 128):
                    o_ref[t * (tn // 128) + sp, 0] = y[:, sp * 128:(sp + 1) * 128]
            else:
                o_ref[0, :, t * tn:(t + 1) * tn] = y

    in_specs = [pl.BlockSpec((1, RB, wd), functools.partial(lambda b, i, cb: (b, i, cb), cb=cb))
                for (_, wd, cb) in rows]
    args = [a for (a, _, _) in rows]
    if mod is not None:
        in_specs.append(pl.BlockSpec((1, 1, 6, D), lambda b, i: (b, i, 0, 0)))
        args.append(mod)
    for cst in consts:
        in_specs.append(pl.BlockSpec(cst.shape, lambda b, i: (0, 0)))
        args.append(cst)
    in_specs.append(pl.BlockSpec((K, N), lambda b, i: (0, 0)))
    args.append(w)
    if split_out:
        out_specs = pl.BlockSpec((N // 128, 1, RB, 128), lambda b, i: (0, b, i, 0))
        out_shape = jax.ShapeDtypeStruct((N // 128, B, NT, 128), out_dtype)
    else:
        out_specs = pl.BlockSpec((1, RB, N), lambda b, i: (b, i, 0))
        out_shape = jax.ShapeDtypeStruct((B, NT, N), out_dtype)
    return pl.pallas_call(
        kern,
        grid=(B, NT // RB),
        in_specs=in_specs,
        out_specs=out_specs,
        out_shape=out_shape,
        compiler_params=_cp(("parallel", "parallel")),
        name=name,
    )(*args)


def _pro_modulate(rv, cv, mv, *, shift, scale):
    return rv[0] * (1.0 + mv(scale)) + mv(shift)


def _inproj(h, mod, w, *, out_dtype=F32, ffn=False, split_out=False, name):
    pro = functools.partial(_pro_modulate, shift=3 if ffn else 0, scale=4 if ffn else 1)
    return _rowmm([(h, D, 0)], [], w, pro, out_dtype=out_dtype, mod=mod, split_out=split_out, name=name)


def _head_norm(x, nheads, hd):
    outs = []
    for hh in range(nheads):
        xh = x[:, hh * hd:(hh + 1) * hd]
        mu = jnp.mean(xh, axis=-1, keepdims=True)
        dd = xh - mu
        var = jnp.mean(dd * dd, axis=-1, keepdims=True)
        outs.append(dd * lax.rsqrt(var + LN_EPS))
    return jnp.concatenate(outs, axis=-1)


def _pro_first(rv, cv, mv):
    return rv[0]


def _epi_ln(y, rv, cv, mv):
    return _layer_norm(ALPHA * rv[-1] + mv(2) * y, cv[-2], cv[-1])


def _conv_silu(src, ctile0, nct, w, b, nctx):
    B, NT, _ = src.shape
    CW = 256
    K = w.shape[0]

    def kern(x_ref, w_ref, b_ref, o_ref, scr):
        scr[0:8, :] = jnp.zeros((8, CW), F32)
        scr[8 + NT:16 + NT, :] = jnp.zeros((8, CW), F32)
        scr[8:8 + NT, :] = x_ref[0]
        t = lax.broadcasted_iota(jnp.int32, (NT, CW), 0)
        seg = t >= nctx
        acc = jnp.zeros((NT, CW), F32)
        for k in range(K):
            o = k - K // 2
            win = scr[8 + o:8 + o + NT, :]
            tt = t + o
            valid = (tt >= 0) & (tt < NT) & ((tt >= nctx) == seg)
            acc = acc + w_ref[k:k + 1, :] * jnp.where(valid, win, 0.0)
        o_ref[0] = _silu(acc + b_ref[...])

    return pl.pallas_call(
        kern,
        grid=(B, nct),
        in_specs=[pl.BlockSpec((1, NT, CW), lambda bb, c: (bb, 0, ctile0 + c)),
                  pl.BlockSpec((K, CW), lambda bb, c: (0, c)),
                  pl.BlockSpec((1, CW), lambda bb, c: (0, c))],
        out_specs=pl.BlockSpec((1, NT, CW), lambda bb, c: (bb, 0, c)),
        out_shape=jax.ShapeDtypeStruct((B, NT, nct * CW), F32),
        scratch_shapes=[pltpu.VMEM((NT + 16, CW), F32)],
        compiler_params=_cp(("parallel", "parallel")),
        name="conv_silu",
    )(src, w, b.reshape(1, -1))


def _chunk_index(d, c, ncc, nc):
    bwd = jnp.where(c < ncc, ncc - 1 - c, nc - 1 - (c - ncc))
    return jnp.where(d == 0, c, bwd)


def _tri_masks():
    r = jnp.arange(CH)
    lower = (r[None, :] <= r[:, None]).astype(F32)
    return jnp.stack([lower, lower.T])


def _mlstm_scan(qk, v, gates, tri, nctx):
    B, NT, _ = v.shape
    nc, ncc = NT // CH, nctx // CH
    HD = A_HD
    scale = HD ** -0.5

    def kern(q_ref, k_ref, v_ref, g_ref, tri_ref, o_ref, ct_ref, n_ref, m_ref):
        @pl.when(pl.program_id(2) == 0)
        def _():
            ct_ref[...] = jnp.zeros(ct_ref.shape, F32)
            n_ref[...] = jnp.zeros(n_ref.shape, F32)
            m_ref[...] = jnp.zeros(m_ref.shape, F32)

        G = g_ref[0]
        tri_m = tri_ref[0]
        mask = tri_m > 0.5
        lane = lax.broadcasted_iota(jnp.int32, (CH, 128), 1)
        Glf = jnp.where((lane >= A_HEADS) & (lane < 2 * A_HEADS), _log_sigmoid(G), 0.0)
        cum_c = _dot_hi(tri_m, Glf)
        GT = G.T
        cum_r = _dot_nt_hi(Glf.T, tri_m)
        tot = jnp.sum(Glf, axis=0, keepdims=True)
        for hh in range(A_HEADS):
            sl = slice(hh * HD, (hh + 1) * HD)
            q = q_ref[0, :, sl] * scale
            k = k_ref[0, :, sl]
            vv = v_ref[0, :, sl]
            m = jnp.max(m_ref[hh:hh + 1, :], axis=1, keepdims=True)
            ig_c = _col(G, hh)
            b_c = _col(cum_c, A_HEADS + hh)
            ig_r = _row(GT, hh)
            b_r = _row(cum_r, A_HEADS + hh)
            logw = jnp.where(mask, b_c - b_r + ig_r, NEG)
            inter = b_c + m
            m_t = jnp.maximum(inter, jnp.max(logw, axis=1, keepdims=True))
            s = _dot_nt(q, k) * jnp.exp(logw - m_t)
            a = jnp.exp(inter - m_t)
            num = _dot(s, vv) + a * _dot(q, ct_ref[hh])
            den = jnp.sum(s, axis=1, keepdims=True) + a * jnp.sum(q * n_ref[hh:hh + 1, :], axis=1, keepdims=True)
            o_ref[0, :, sl] = num / jnp.maximum(jnp.abs(den), jnp.exp(-m_t))
            b_last = _col(tot, A_HEADS + hh)
            g_r = b_last - b_r + ig_r
            g_c = b_last - b_c + ig_c
            m_new = jnp.maximum(b_last + m, jnp.max(g_r, axis=1, keepdims=True))
            decay = jnp.exp(b_last + m - m_new)
            kw = k * jnp.exp(g_c - m_new)
            ct_ref[hh] = decay * ct_ref[hh] + _dot_tn(kw, vv)
            n_ref[hh:hh + 1, :] = decay * n_ref[hh:hh + 1, :] + jnp.sum(kw, axis=0, keepdims=True)
            m_ref[hh:hh + 1, :] = jnp.broadcast_to(m_new, (1, 128))

    ci = functools.partial(_chunk_index, ncc=ncc, nc=nc)
    W = A_HEADS * HD
    return pl.pallas_call(
        kern,
        grid=(B, 2, nc),
        in_specs=[pl.BlockSpec((1, CH, W), lambda b, d, c: (b, ci(d, c), 0)),
                  pl.BlockSpec((1, CH, W), lambda b, d, c: (b, ci(d, c), 1)),
                  pl.BlockSpec((1, CH, W), lambda b, d, c: (b, ci(d, c), 0)),
                  pl.BlockSpec((1, CH, 128), lambda b, d, c: (b, ci(d, c), d)),
                  pl.BlockSpec((1, CH, CH), lambda b, d, c: (d, 0, 0))],
        out_specs=pl.BlockSpec((1, CH, W), lambda b, d, c: (b, ci(d, c), d)),
        out_shape=jax.ShapeDtypeStruct((B, NT, 2 * W), F32),
        scratch_shapes=[pltpu.VMEM((A_HEADS, HD, HD), F32), pltpu.VMEM((8, HD), F32), pltpu.VMEM((8, 128), F32)],
        compiler_params=_cp(("parallel", "arbitrary", "arbitrary")),
        name="mlstm_scan",
    )(qk, qk, v, gates, tri)


def _rope_blocks(x, cos, sin, half):
    per = cos.shape[1] // 128
    lane = lax.broadcasted_iota(jnp.int32, (x.shape[0], 128), 1)
    outs = []
    for c in range(x.shape[1] // 128):
        xb = x[:, c * 128:(c + 1) * 128]
        cb = cos[:, (c % per) * 128:(c % per + 1) * 128]
        sb = sin[:, (c % per) * 128:(c % per + 1) * 128]
        if half == 64:
            partner = pltpu.roll(xb, 64, 1)
        else:
            lo = (lane % (2 * half)) < half
            partner = jnp.where(lo, pltpu.roll(xb, 128 - half, 1), pltpu.roll(xb, half, 1))
        outs.append(xb * cb + partner * sb)
    return jnp.concatenate(outs, axis=-1)


def _retention_scan(proj, cos, sin, dec, qd, kd, cd, nctx):
    B, NT, _ = proj.shape
    nc, ncc = NT // CH, nctx // CH
    scale = D_QK ** -0.5

    def kern(q_ref, k_ref, v_ref, cos_ref, sin_ref, dec_ref, qd_ref, kd_ref, cd_ref, o_ref, s_ref):
        @pl.when(pl.program_id(2) == 0)
        def _():
            s_ref[...] = jnp.zeros(s_ref.shape, F32)

        cos_t, sin_t = cos_ref[...], sin_ref[...]
        qd_t, kd_t = qd_ref[0], kd_ref[0]
        cd_t = cd_ref[0, 0:1, :]
        for hh in range(D_HEADS):
            qh = _rope_blocks(q_ref[0, :, hh * D_QK:(hh + 1) * D_QK], cos_t, sin_t, 64) * scale
            kh = _rope_blocks(k_ref[0, :, hh * D_QK:(hh + 1) * D_QK], cos_t, sin_t, 64)
            vh = v_ref[0, :, hh * D_V:(hh + 1) * D_V]
            s = _dot_nt(qh, kh) * dec_ref[0, hh]
            o_ref[0, :, hh * D_V:(hh + 1) * D_V] = _dot(s, vh) + _dot(qh, s_ref[hh]) * _col(qd_t, hh)
            s_ref[hh] = _col(cd_t, hh) * s_ref[hh] + _dot_tn(kh * _col(kd_t, hh), vh)

    ci = functools.partial(_chunk_index, ncc=ncc, nc=nc)
    WV = D_HEADS * D_V
    return pl.pallas_call(
        kern,
        grid=(B, 2, nc),
        in_specs=[pl.BlockSpec((1, CH, D), lambda b, d, c: (b, ci(d, c), 0)),
                  pl.BlockSpec((1, CH, D), lambda b, d, c: (b, ci(d, c), 1)),
                  pl.BlockSpec((1, CH, WV), lambda b, d, c: (b, ci(d, c), 1)),
                  pl.BlockSpec((CH, 256), lambda b, d, c: (ci(d, c), 0)),
                  pl.BlockSpec((CH, 256), lambda b, d, c: (ci(d, c), 0)),
                  pl.BlockSpec((1, D_HEADS, CH, CH), lambda b, d, c: (d, 0, 0, 0)),
                  pl.BlockSpec((1, CH, 128), lambda b, d, c: (d, 0, 0)),
                  pl.BlockSpec((1, CH, 128), lambda b, d, c: (d, 0, 0)),
                  pl.BlockSpec((1, 8, 128), lambda b, d, c: (d, 0, 0))],
        out_specs=pl.BlockSpec((1, CH, WV), lambda b, d, c: (b, ci(d, c), d)),
        out_shape=jax.ShapeDtypeStruct((B, NT, 2 * WV), F32),
        scratch_shapes=[pltpu.VMEM((D_HEADS, D_QK, D_V), F32)],
        compiler_params=_cp(("parallel", "arbitrary", "arbitrary")),
        name="retention_scan",
    )(proj, proj, proj, cos, sin, dec, qd, kd, cd)


def _ssd_scan(xbc, dtraw, dt_bias, a_rate, expand, tri, nctx):
    B, NT, _ = xbc.shape
    nc, ncc = NT // CH, nctx // CH
    R = B_HEADS // B_GROUPS
    GW = R * B_HD
    DI = B_HEADS * B_HD

    def kern(x_ref, b_ref, c_ref, dt_ref, bias_ref, a_ref, e_ref, tri_ref, o_ref, st_ref):
        @pl.when(pl.program_id(2) == 0)
        def _():
            st_ref[...] = jnp.zeros(st_ref.shape, F32)

        tri_m = tri_ref[0]
        mask = tri_m > 0.5
        E = e_ref[...]
        dt = _softplus(dt_ref[0] + bias_ref[0])
        dta = dt * a_ref[0]
        cum_c = _dot_hi(tri_m, dta)
        cum_r = _dot_nt_hi(dta.T, tri_m)
        a_last = jnp.sum(dta, axis=0, keepdims=True)
        xdt = x_ref[0] * _expand(dt, E)
        eacs = _expand(jnp.exp(cum_c), E)
        wst = _expand(jnp.exp(a_last - cum_c), E)
        da = _expand(jnp.broadcast_to(jnp.exp(a_last), (8, 128)), E)
        lane = lax.broadcasted_iota(jnp.int32, (CH, 128), 1)
        lo = lane < B_HD
        for g in range(B_GROUPS):
            bg = b_ref[0, :, g * B_STATE:(g + 1) * B_STATE]
            cg = c_ref[0, :, g * B_STATE:(g + 1) * B_STATE]
            cb = _dot_nt(cg, bg)
            xg = xdt[:, g * GW:(g + 1) * GW]
            y2 = _dot(cg, st_ref[g]) * eacs[:, g * GW:(g + 1) * GW]
            for pr in range(R // 2):
                xp = xg[:, pr * 128:(pr + 1) * 128]
                acc = y2[:, pr * 128:(pr + 1) * 128]
                for u in range(2):
                    hd = g * R + pr * 2 + u
                    seg = jnp.exp(jnp.where(mask, _col(cum_c, hd) - _row(cum_r, hd), NEG))
                    xm = jnp.where(lo if u == 0 else jnp.logical_not(lo), xp, 0.0)
                    acc = acc + _dot(cb * seg, xm)
                o_ref[0, :, g * GW + pr * 128:g * GW + (pr + 1) * 128] = acc
            st_ref[g] = da[0:1, g * GW:(g + 1) * GW] * st_ref[g] + _dot_tn(bg, xg * wst[:, g * GW:(g + 1) * GW])

    ci = functools.partial(_chunk_index, ncc=ncc, nc=nc)
    GS = B_GROUPS * B_STATE
    return pl.pallas_call(
        kern,
        grid=(B, 2, nc),
        in_specs=[pl.BlockSpec((1, CH, DI), lambda b, d, c: (b, ci(d, c), 0)),
                  pl.BlockSpec((1, CH, GS), lambda b, d, c: (b, ci(d, c), 2)),
                  pl.BlockSpec((1, CH, GS), lambda b, d, c: (b, ci(d, c), 3)),
                  pl.BlockSpec((1, CH, 128), lambda b, d, c: (b, ci(d, c), d)),
                  pl.BlockSpec((1, 1, 128), lambda b, d, c: (d, 0, 0)),
                  pl.BlockSpec((1, 1, 128), lambda b, d, c: (d, 0, 0)),
                  pl.BlockSpec((128, DI), lambda b, d, c: (0, 0)),
                  pl.BlockSpec((1, CH, CH), lambda b, d, c: (d, 0, 0))],
        out_specs=pl.BlockSpec((1, CH, DI), lambda b, d, c: (b, ci(d, c), d)),
        out_shape=jax.ShapeDtypeStruct((B, NT, 2 * DI), F32),
        scratch_shapes=[pltpu.VMEM((B_GROUPS, B_STATE, GW), F32)],
        compiler_params=_cp(("parallel", "arbitrary", "arbitrary")),
        name="ssd_scan",
    )(xbc, xbc, xbc, dtraw, dt_bias, a_rate, expand, tri)


def _attn_prep(qkv, cos, sin):
    B, NT, _ = qkv.shape
    scale = C_HD ** -0.5 * math.log2(math.e)

    def kern(x_ref, cos_ref, sin_ref, o_ref):
        j = pl.program_id(2)
        x = x_ref[0]

        @pl.when(j < 2)
        def _():
            y = _rope_blocks(x, cos_ref[...], sin_ref[...], C_HD // 4)
            o_ref[0] = (y * jnp.where(j == 0, scale, 1.0)).astype(BF)

        @pl.when(j == 2)
        def _():
            o_ref[0] = x.astype(BF)

    return pl.pallas_call(
        kern,
        grid=(B, NT // RB, 3),
        in_specs=[pl.BlockSpec((1, RB, D), lambda b, i, j: (b, i, j)),
                  pl.BlockSpec((RB, 128), lambda b, i, j: (i, 0)),
                  pl.BlockSpec((RB, 128), lambda b, i, j: (i, 0))],
        out_specs=pl.BlockSpec((1, RB, D), lambda b, i, j: (b, i, j)),
        out_shape=jax.ShapeDtypeStruct((B, NT, 3 * D), BF),
        compiler_params=_cp(("parallel", "parallel", "arbitrary")),
        name="attn_prep",
    )(qkv, cos, sin)


def _diff_attention(qkvb, lam, norm_g, out_scale, nctx):
    B, NT, _ = qkvb.shape
    TQ = RB
    HP = 2 * C_HD

    def kern(q_ref, k_ref, v_ref, lam_ref, g_ref, o_ref):
        lam_v = lam_ref[0:1, 0:1]
        lane = lax.broadcasted_iota(jnp.int32, (TQ, HP), 1)
        lo = lane < C_HD

        def attend(nk):
            for hp in range(C_HEADS):
                qp = q_ref[0, :, hp * HP:(hp + 1) * HP]
                kp = k_ref[0, 0:nk, hp * HP:(hp + 1) * HP]
                vp = v_ref[0, 0:nk, hp * HP:(hp + 1) * HP]
                outs = []
                for u in range(2):
                    qm = jnp.where(lo if u == 0 else jnp.logical_not(lo), qp, jnp.zeros_like(qp))
                    s = lax.dot_general(qm, kp, (((1,), (1,)), ((), ())), preferred_element_type=F32)
                    p = jnp.exp2(s - jnp.max(s, axis=1, keepdims=True))
                    z = jnp.sum(p, axis=1, keepdims=True)
                    outs.append(jnp.dot(p.astype(BF), vp, preferred_element_type=F32) / z)
                o = outs[0] - lam_v * outs[1]
                o = o * lax.rsqrt(jnp.mean(o * o, axis=-1, keepdims=True) + RMS_EPS)
                o_ref[0, :, hp * HP:(hp + 1) * HP] = o * g_ref[...] * out_scale

        is_ctx = pl.program_id(1) * TQ < nctx

        @pl.when(is_ctx)
        def _():
            attend(nctx)

        @pl.when(jnp.logical_not(is_ctx))
        def _():
            attend(NT)

    return pl.pallas_call(
        kern,
        grid=(B, NT // TQ),
        in_specs=[pl.BlockSpec((1, TQ, D), lambda b, i: (b, i, 0)),
                  pl.BlockSpec((1, NT, D), lambda b, i: (b, 0, 1)),
                  pl.BlockSpec((1, NT, D), lambda b, i: (b, 0, 2)),
                  pl.BlockSpec((8, 128), lambda b, i: (0, 0)),
                  pl.BlockSpec((1, HP), lambda b, i: (0, 0))],
        out_specs=pl.BlockSpec((1, TQ, D), lambda b, i: (b, i, 0)),
        out_shape=jax.ShapeDtypeStruct((B, NT, D), F32),
        compiler_params=_cp(("parallel", "arbitrary")),
        name="diff_attention",
    )(qkvb, qkvb, qkvb, lam, norm_g.reshape(1, HP))


def _sort16_network():
    n, pairs, p = 16, [], 1
    while p < n:
        k = p
        while k >= 1:
            for j in range(k % p, n - k, 2 * k):
                for i in range(min(k, n - j - k)):
                    if (i + j) // (2 * p) == (i + j + k) // (2 * p):
                        pairs.append((i + j, i + j + k))
            k //= 2
        p *= 2
    return pairs


def _top16_desc(s):
    v = [s[8 * k:8 * (k + 1), :] for k in range(16)]
    for a, b in _sort16_network():
        v[a], v[b] = jnp.maximum(v[a], v[b]), jnp.minimum(v[a], v[b])
    tops = []
    for r in range(P_TOPK):
        m = jnp.max(v[0], axis=0, keepdims=True)
        tops.append(m)
        if r + 1 < P_TOPK:
            hit = v[0] == m
            for k in range(P_TOPK - 1 - r):
                v[k] = jnp.where(hit, v[k + 1], v[k])
    return tops


def _peer_ln(h, q16, mod, keys, u, vt, ln_g, ln_b):
    B, NT, _ = h.shape
    TS, TE = P_TS, P_TE
    E = u.shape[0]
    ne = E // TE
    NTB = TS // 128
    NIB = TE // P_NK
    RPB = TS // RB
    hf = h.reshape(B * NT, D)
    modf = mod.reshape(B * (NT // RB), 6, D)
    pairs = [(ra, rb) for ra in range(P_TOPK) for rb in range(P_TOPK // (ra + 1))]
    NCAND = 8 * ((len(pairs) + 7) // 8)

    def kern(h_ref, q_ref, mod_ref, keys_ref, u_ref, vt_ref, g_ref, b_ref, o_ref,
             xin, rbk, e2m, lr, f1, cand, a_sc, w_sc, acc):
        e = pl.program_id(1)

        @pl.when(e == 0)
        def _():
            for r in range(RPB):
                hx = h_ref[r * RB:(r + 1) * RB, :]
                xin[r * RB:(r + 1) * RB, :] = (hx * (1.0 + mod_ref[r, 4:5, :]) + mod_ref[r, 3:4, :]).astype(BF)
            acc[...] = jnp.zeros(acc.shape, F32)

            def head_body(hh, carry):
                for tb in range(NTB):
                    tsl = slice(tb * 128, (tb + 1) * 128)
                    vals = []
                    for c in range(2):
                        s = lax.dot_general(keys_ref[2 * hh + c], q_ref[2 * hh + c, tsl, :],
                                            (((1,), (1,)), ((), ())), preferred_element_type=F32)
                        vals.append((s, _top16_desc(s)))
                    (s1, ta), (s2, tbv) = vals
                    cand[...] = jnp.full(cand.shape, NEG, F32)
                    for n, (pa, pb) in enumerate(pairs):
                        cand[n:n + 1, :] = ta[pa] + tbv[pb]
                    cv = cand[...]
                    work = cv
                    for r in range(P_TOPK):
                        th = jnp.max(work, axis=0, keepdims=True)
                        if r + 1 < P_TOPK:
                            work = jnp.where(work == th, NEG, work)
                    top = ta[0] + tbv[0]
                    z = jnp.sum(jnp.where(cv >= th, jnp.exp(cv - top), 0.0), axis=0, keepdims=True)
                    lrv = jnp.zeros(s1.shape, F32)
                    rb = jnp.zeros(s2.shape, F32)
                    for pa in range(P_TOPK):
                        cnt = jnp.zeros(th.shape, F32)
                        for pb in range(P_TOPK // (pa + 1)):
                            cnt = cnt + jnp.where(ta[pa] + tbv[pb] >= th, 1.0, 0.0)
                        lrv = jnp.where(s1 == ta[pa], cnt, lrv)
                        rb = jnp.where(s2 < tbv[pa], float(pa + 1), rb)
                    lr[hh, :, tsl] = lrv
                    f1[hh, :, tsl] = jnp.where(s1 >= ta[-1], jnp.exp(s1 - ta[0]), 0.0)
                    rbk[hh, :, tsl] = rb.astype(BF)
                    e2m[hh, :, tsl] = (jnp.where(s2 >= tbv[-1], jnp.exp(s2 - tbv[0]), 0.0) / z).astype(BF)
                return carry

            lax.fori_loop(0, P_HEADS, head_body, 0)

        a_sc[...] = lax.dot_general(u_ref[...], xin[...], (((1,), (1,)), ((), ())), preferred_element_type=F32)
        i8 = pl.multiple_of(e * NIB, 8)
        zero = jnp.zeros((), BF)
        for tb in range(NTB):
            tsl = slice(tb * 128, (tb + 1) * 128)
            lrb = [lr[hh, pl.ds(i8, NIB), tsl] for hh in range(P_HEADS)]
            f1b = [f1[hh, pl.ds(i8, NIB), tsl] for hh in range(P_HEADS)]
            for k in range(NIB):
                gm = jnp.zeros((P_NK, 128), BF)
                for hh in range(P_HEADS):
                    lrow = lrb[hh][k:k + 1, :].astype(BF)
                    frow = f1b[hh][k:k + 1, :].astype(BF)
                    gm = gm + jnp.where(rbk[hh, :, tsl] < lrow, e2m[hh, :, tsl], zero) * frow
                ksl = slice(k * P_NK, (k + 1) * P_NK)
                w_sc[ksl, tsl] = gm * _gelu(a_sc[ksl, tsl].astype(BF))
        acc[...] += jnp.dot(vt_ref[...], w_sc[...], preferred_element_type=F32)

        @pl.when(e == ne - 1)
        def _():
            y = acc[...].T
            for r in range(RPB):
                hx = h_ref[r * RB:(r + 1) * RB, :]
                o_ref[r * RB:(r + 1) * RB, :] = _layer_norm(
                    ALPHA * hx + mod_ref[r, 5:6, :] * y[r * RB:(r + 1) * RB, :], g_ref[...], b_ref[...])

    out = pl.pallas_call(
        kern,
        grid=(B * NT // TS, ne),
        in_specs=[pl.BlockSpec((TS, D), lambda s, e: (s, 0)),
                  pl.BlockSpec((2 * P_HEADS, TS, P_NK), lambda s, e: (0, s, 0)),
                  pl.BlockSpec((RPB, 6, D), lambda s, e: (s, 0, 0)),
                  pl.BlockSpec((2 * P_HEADS, P_NK, P_NK), lambda s, e: (0, 0, 0)),
                  pl.BlockSpec((TE, D), lambda s, e: (e, 0)),
                  pl.BlockSpec((D, TE), lambda s, e: (0, e)),
                  pl.BlockSpec((1, D), lambda s, e: (0, 0)),
                  pl.BlockSpec((1, D), lambda s, e: (0, 0))],
        out_specs=pl.BlockSpec((TS, D), lambda s, e: (s, 0)),
        out_shape=jax.ShapeDtypeStruct((B * NT, D), F32),
        scratch_shapes=[pltpu.VMEM((TS, D), BF),
                        pltpu.VMEM((P_HEADS, P_NK, TS), BF), pltpu.VMEM((P_HEADS, P_NK, TS), BF),
                        pltpu.VMEM((P_HEADS, P_NK, TS), F32), pltpu.VMEM((P_HEADS, P_NK, TS), F32),
                        pltpu.VMEM((NCAND, 128), F32),
                        pltpu.VMEM((TE, TS), F32), pltpu.VMEM((TE, TS), BF),
                        pltpu.VMEM((D, TS), F32)],
        compiler_params=_cp(("parallel", "arbitrary")),
        name="peer",
    )(hf, q16, modf, keys, u, vt, ln_g.reshape(1, D), ln_b.reshape(1, D))
    return out.reshape(B, NT, D)


def _rope_table(nctx, nlat, half, reps):
    t = jnp.arange(nlat, dtype=jnp.int32)
    freqs = ROPE_BASE ** (-jnp.arange(half, dtype=F32) / half)
    sign = jnp.concatenate([-jnp.ones((half,), F32), jnp.ones((half,), F32)])
    cs, sn = [], []
    for pos in (t // GRID_W, t % GRID_W):
        ang = pos.astype(F32)[:, None] * freqs
        cs.append(jnp.concatenate([jnp.cos(ang), jnp.cos(ang)], axis=-1))
        sn.append(jnp.concatenate([jnp.sin(ang), jnp.sin(ang)], axis=-1) * sign)
    cos = jnp.tile(jnp.concatenate(cs, axis=-1), (1, reps))
    sin = jnp.tile(jnp.concatenate(sn, axis=-1), (1, reps))
    cos = jnp.concatenate([jnp.ones((nctx, cos.shape[1]), F32), cos], axis=0)
    sin = jnp.concatenate([jnp.zeros((nctx, sin.shape[1]), F32), sin], axis=0)
    return cos, sin


def _retention_tables(decay_logit):
    lg = jax.nn.log_sigmoid(decay_logit.astype(F32))
    pos = jnp.arange(CH, dtype=F32)
    rel = pos[:, None] - pos[None, :]
    lower = jnp.where(rel >= 0, jnp.exp(jnp.maximum(rel, 0.0) * lg[:, :, None, None]), 0.0)
    dec = jnp.stack([lower[0], jnp.swapaxes(lower[1], -1, -2)])
    qd_f = jnp.exp((pos + 1.0)[:, None] * lg[0][None, :])
    kd_f = jnp.exp((CH - 1.0 - pos)[:, None] * lg[0][None, :])
    qd_b = jnp.exp((CH - pos)[:, None] * lg[1][None, :])
    kd_b = jnp.exp(pos[:, None] * lg[1][None, :])
    pad = lambda a: jnp.pad(a, ((0, 0), (0, 128 - a.shape[1])))
    qd = jnp.stack([pad(qd_f), pad(qd_b)])
    kd = jnp.stack([pad(kd_f), pad(kd_b)])
    cd = jnp.broadcast_to(jnp.pad(jnp.exp(CH * lg), ((0, 0), (0, 128 - D_HEADS)))[:, None, :], (2, 8, 128))
    return dec, qd, kd, cd


def _mlstm_layer(h, mod, lnc, tri, nctx, w_up, conv_w, conv_b, w_qk, w_v, w_gate, b_gate, norm_g, skip, w_down):
    DI = A_HEADS * A_HD
    up = _inproj(h, mod, w_up.astype(BF), name="mlstm_up")
    xc = _conv_silu(up, 0, DI // 256, conv_w, conv_b, nctx)
    qk = _rowmm([(xc, DI, 0)], [], w_qk.astype(BF), _pro_first, out_dtype=F32, name="mlstm_qk")
    v = _rowmm([(up, DI, 0)], [], w_v.astype(BF), _pro_first, out_dtype=F32, name="mlstm_v")
    wg = w_gate.reshape(3 * DI, 4, A_HEADS)
    bg = b_gate.reshape(4, A_HEADS)
    zw = jnp.zeros((3 * DI, 128 - 2 * A_HEADS), F32)
    zb = jnp.zeros((128 - 2 * A_HEADS,), F32)
    wg2 = jnp.concatenate([wg[:, 0], wg[:, 1], zw, wg[:, 2], wg[:, 3], zw], axis=1).astype(BF)
    bg2 = jnp.concatenate([bg[0], bg[1], zb, bg[2], bg[3], zb]).reshape(1, 256)

    def gate_epi(y, rv, cv, mv):
        return y + _dot(rv[1], cv[0]) + _dot(rv[2], cv[1]) + cv[2]

    gates = _rowmm([(v, DI, 0), (qk, DI, 0), (qk, DI, 1)], [wg2[:DI], wg2[DI:2 * DI], bg2], wg2[2 * DI:],
                   _pro_first, gate_epi, out_dtype=F32, name="mlstm_gates")
    hs = _mlstm_scan(qk, v, gates, tri, nctx)

    def out_pro(rv, cv, mv):
        hf, hb, z, o_pre, xcv, _ = rv
        hh = _sigmoid(o_pre) * (hf + hb)
        return (_head_norm(hh, A_HEADS, A_HD) * cv[0] + cv[1] * xcv) * _silu(z)

    return _rowmm([(hs, DI, 0), (hs, DI, 1), (up, DI, 1), (up, DI, 2), (xc, DI, 0), (h, D, 0)],
                  [norm_g.reshape(1, DI), skip.reshape(1, DI)] + lnc, w_down.astype(BF), out_pro, _epi_ln,
                  out_dtype=F32, mod=mod, name="mlstm_out")


def _ssd_layer(h, mod, lnc, tri, nctx, w_in, conv_w, conv_b, dt_bias, a_log, d_skip, norm_g, w_out):
    DI = B_HEADS * B_HD
    CC = DI + 2 * B_GROUPS * B_STATE
    proj = _inproj(h, mod, w_in[:, :DI + CC].astype(BF), name="ssd_in")
    wdt = w_in[:, DI + CC:].reshape(D, 2, B_HEADS)
    zpad = jnp.zeros((D, 128 - B_HEADS), F32)
    wdt2 = jnp.concatenate([wdt[:, 0], zpad, wdt[:, 1], zpad], axis=1).astype(BF)
    dtraw = _inproj(h, mod, wdt2, name="ssd_dt")
    xbc = _conv_silu(proj, DI // 256, CC // 256, conv_w, conv_b, nctx)
    padl = lambda a: jnp.pad(a.astype(F32), ((0, 0), (0, 128 - B_HEADS))).reshape(2, 1, 128)
    expand = (jnp.arange(128)[:, None] == (jnp.arange(DI)[None, :] // B_HD)).astype(BF)
    ys = _ssd_scan(xbc, dtraw, padl(dt_bias), padl(-jnp.exp(a_log.astype(F32))), expand, tri, nctx)
    dvec = jnp.repeat(d_skip.astype(F32), B_HD).reshape(1, DI)
    GW = DI // B_GROUPS

    def out_pro(rv, cv, mv):
        yf, yb, xs, z, _ = rv
        y = (yf + yb + cv[0] * xs) * _silu(z)
        outs = []
        for g in range(B_GROUPS):
            yg = y[:, g * GW:(g + 1) * GW]
            outs.append(yg * lax.rsqrt(jnp.mean(yg * yg, axis=-1, keepdims=True) + RMS_EPS))
        return jnp.concatenate(outs, axis=-1) * cv[1]

    return _rowmm([(ys, DI, 0), (ys, DI, 1), (xbc, DI, 0), (proj, DI, 0), (h, D, 0)],
                  [dvec, norm_g.reshape(1, DI)] + lnc, w_out.astype(BF), out_pro, _epi_ln,
                  out_dtype=F32, mod=mod, name="ssd_out")


def _diff_layer(h, mod, lnc, nctx, layer_idx, w_qkv, lam_vecs, norm_g, w_out):
    B, NT, _ = h.shape
    lam_init = 0.8 - 0.6 * math.exp(-0.3 * layer_idx)
    lv = lam_vecs.astype(F32)
    lam = jnp.exp(jnp.sum(lv[0] * lv[1])) - jnp.exp(jnp.sum(lv[2] * lv[3])) + lam_init
    qkv = _inproj(h, mod, w_qkv.astype(BF), name="diff_qkv")
    cos, sin = _rope_table(nctx, NT - nctx, C_HD // 4, 2)
    qkvb = _attn_prep(qkv, cos, sin)
    o = _diff_attention(qkvb, jnp.full((8, 128), lam, F32), norm_g, 1.0 - lam_init, nctx)
    return _rowmm([(o, D, 0), (h, D, 0)], lnc, w_out.astype(BF), _pro_first, _epi_ln, out_dtype=F32, mod=mod,
                  name="diff_out")


def _retention_layer(h, mod, lnc, nctx, w_in, decay_logit, norm_g, w_out):
    B, NT, _ = h.shape
    DV = D_HEADS * D_V
    proj = _inproj(h, mod, w_in.astype(BF), name="ret_in")
    cos, sin = _rope_table(nctx, NT - nctx, D_QK // 4, 1)
    dec, qd, kd, cd = _retention_tables(decay_logit)
    os_ = _retention_scan(proj, cos, sin, dec, qd, kd, cd, nctx)

    def out_pro(rv, cv, mv):
        of, ob, g, _ = rv
        return _head_norm(of + ob, D_HEADS, D_V) * cv[0] * _silu(g)

    return _rowmm([(os_, DV, 0), (os_, DV, 1), (proj, DV, 2), (h, D, 0)], [norm_g.reshape(1, DV)] + lnc,
                  w_out.astype(BF), out_pro, _epi_ln, out_dtype=F32, mod=mod, name="ret_out")


def kernel(x, c, ctx, c_ctx, ada_w, ada_b, ln_g, ln_b, peer_wq, peer_keys, peer_u, peer_v, mlstm_w_up, mlstm_conv_w, mlstm_conv_b, mlstm_w_qk, mlstm_w_v, mlstm_w_gate, mlstm_b_gate, mlstm_norm_g, mlstm_skip, mlstm_w_down, ssd_w_in, ssd_conv_w, ssd_conv_b, ssd_dt_bias, ssd_a_log, ssd_d, ssd_norm_g, ssd_w_out, diff_w_qkv, diff_lambda, diff_norm_g, diff_w_out, ret_w_in, ret_decay_logit, ret_norm_g, ret_w_out):
    B, nlat, _ = x.shape
    nctx = ctx.shape[1]
    NT = nctx + nlat
    nrb = NT // RB
    h = jnp.concatenate([ctx, x], axis=1)
    rows = 8 * ((B + 1 + 7) // 8)
    cond = jnp.concatenate([c, c_ctx[None, :], jnp.zeros((rows - B - 1, D), F32)], axis=0)
    modall = _modulation(cond, ada_w, ada_b)
    tri = _tri_masks()
    for i in range(DEPTH):
        kind, j = i % 4, i // 4
        ml = modall[i, :B].reshape(B, 1, 6, D)
        mc = modall[i, B].reshape(1, 1, 6, D)
        mod = jnp.concatenate([jnp.broadcast_to(mc, (B, nctx // RB, 6, D)),
                               jnp.broadcast_to(ml, (B, nlat // RB, 6, D))], axis=1).reshape(B, nrb, 6, D)
        lnc = [ln_g[i, 0].reshape(1, D), ln_b[i, 0].reshape(1, D)]
        if kind == 0:
            h = _mlstm_layer(h, mod, lnc, tri, nctx, mlstm_w_up[j], mlstm_conv_w[j], mlstm_conv_b[j], mlstm_w_qk[j],
                             mlstm_w_v[j], mlstm_w_gate[j], mlstm_b_gate[j], mlstm_norm_g[j], mlstm_skip[j],
                             mlstm_w_down[j])
        elif kind == 1:
            h = _ssd_layer(h, mod, lnc, tri, nctx, ssd_w_in[j], ssd_conv_w[j], ssd_conv_b[j], ssd_dt_bias[j],
                           ssd_a_log[j], ssd_d[j], ssd_norm_g[j], ssd_w_out[j])
        elif kind == 2:
            h = _diff_layer(h, mod, lnc, nctx, i, diff_w_qkv[j], diff_lambda[j], diff_norm_g[j], diff_w_out[j])
        else:
            h = _retention_layer(h, mod, lnc, nctx, ret_w_in[j], ret_decay_logit[j], ret_norm_g[j], ret_w_out[j])
        q16 = _inproj(h, mod, peer_wq[i].astype(BF), out_dtype=BF, ffn=True, split_out=True,
                      name="peer_q").reshape(2 * P_HEADS, B * NT, P_NK)
        keys = peer_keys[i].reshape(2 * P_HEADS, P_NK, P_NK).astype(BF)
        h = _peer_ln(h, q16, mod, keys, peer_u[i].astype(BF), peer_v[i].astype(BF).T, ln_g[i, 1], ln_b[i, 1])
    return h[:, nctx:, :]
```

```python
import functools
import math

import jax
import jax.numpy as jnp
from jax import lax
from jax.experimental import pallas as pl
from jax.experimental.pallas import tpu as pltpu

F32 = jnp.float32
BF = jnp.bfloat16
HI = lax.Precision.HIGHEST

D = 1024
DEPTH = 4
GRID_W = 64
CH = 128
RB = 256
LN_EPS = 1e-5
RMS_EPS = 1e-6
ALPHA = (2.0 * DEPTH) ** 0.25
ROPE_BASE = 10000.0
NEG = -jnp.inf

A_HEADS, A_HD = 4, 512
B_HEADS, B_HD, B_GROUPS, B_STATE = 32, 64, 8, 128
C_HEADS, C_HD = 8, 64
D_HEADS, D_QK, D_V = 4, 256, 512
P_HEADS, P_NK, P_TOPK = 8, 128, 16
P_TS = 768
P_TS_LAST = 512
P_TE = 1024

VMEM_LIMIT = 56 * 1024 * 1024


def _cp(sem):
    return pltpu.CompilerParams(dimension_semantics=sem, vmem_limit_bytes=VMEM_LIMIT)


def _dot(a, b):
    return jnp.dot(a.astype(BF), b.astype(BF), preferred_element_type=F32)


def _dot_nt(a, b):
    return lax.dot_general(a.astype(BF), b.astype(BF), (((1,), (1,)), ((), ())),
                           preferred_element_type=F32)


def _dot_tn(a, b):
    return lax.dot_general(a.astype(BF), b.astype(BF), (((0,), (0,)), ((), ())),
                           preferred_element_type=F32)


def _dot_hi(a, b):
    return jnp.dot(a, b, preferred_element_type=F32, precision=HI)


def _dot_nt_hi(a, b):
    return lax.dot_general(a, b, (((1,), (1,)), ((), ())), preferred_element_type=F32, precision=HI)


def _expand(x, e01):
    hi = x.astype(BF)
    lo = (x - hi.astype(F32)).astype(BF)
    return (jnp.dot(hi, e01.astype(BF), preferred_element_type=F32)
            + jnp.dot(lo, e01.astype(BF), preferred_element_type=F32))


def _col(x, idx):
    lane = lax.broadcasted_iota(jnp.int32, x.shape, 1)
    return jnp.sum(jnp.where(lane == idx, x, 0.0), axis=1, keepdims=True)


def _row(x, idx):
    sub = lax.broadcasted_iota(jnp.int32, x.shape, 0)
    return jnp.sum(jnp.where(sub == idx, x, 0.0), axis=0, keepdims=True)


def _sigmoid(x):
    return 1.0 / (1.0 + jnp.exp(-x))


def _silu(x):
    return x * _sigmoid(x)


def _softplus(x):
    return jnp.maximum(x, 0.0) + jnp.log(1.0 + jnp.exp(-jnp.abs(x)))


def _log_sigmoid(x):
    return jnp.minimum(x, 0.0) - jnp.log(1.0 + jnp.exp(-jnp.abs(x)))


def _gelu(x):
    return 0.5 * x * (1.0 + jnp.tanh(math.sqrt(2.0 / math.pi) * (x + 0.044715 * (x * x * x))))


def _layer_norm(r, g, b):
    mu = jnp.mean(r, axis=-1, keepdims=True)
    d = r - mu
    var = jnp.mean(d * d, axis=-1, keepdims=True)
    return d * lax.rsqrt(var + LN_EPS) * g + b


def _modulation(cond, ada_w, ada_b):
    R = cond.shape[0]

    def kern(c_ref, w_ref, b_ref, o_ref):
        o_ref[0] = _dot(_silu(c_ref[...]), w_ref[0]) + b_ref[0]

    return pl.pallas_call(
        kern,
        grid=(DEPTH, 6),
        in_specs=[pl.BlockSpec((R, D), lambda l, j: (0, 0)),
                  pl.BlockSpec((1, D, D), lambda l, j: (l, 0, j)),
                  pl.BlockSpec((1, 1, D), lambda l, j: (l, 0, j))],
        out_specs=pl.BlockSpec((1, R, D), lambda l, j: (l, 0, j)),
        out_shape=jax.ShapeDtypeStruct((DEPTH, R, 6 * D), F32),
        compiler_params=_cp(("parallel", "parallel")),
        name="modulation",
    )(cond, ada_w, ada_b.reshape(DEPTH, 1, 6 * D))


def _rowmm(rows, consts, w, pro, epi=None, *, out_dtype, mod=None, split_out=False, name):
    B, NT = rows[0][0].shape[:2]
    K, N = w.shape
    tn = min(N, 1024)
    assert N % tn == 0 and (epi is None or N == tn)
    nr, nc = len(rows), len(consts)

    def kern(*refs):
        rv = [r[0] for r in refs[:nr]]
        p = nr
        mv = None
        if mod is not None:
            mod_ref = refs[p]
            mv = lambda k: mod_ref[0, 0, k:k + 1, :]
            p += 1
        cv = [r[...] for r in refs[p:p + nc]]
        w_ref, o_ref = refs[p + nc], refs[p + nc + 1]
        xb = pro(rv, cv, mv).astype(BF)
        for t in range(N // tn):
            y = jnp.dot(xb, w_ref[:, t * tn:(t + 1) * tn], preferred_element_type=F32)
            if epi is not None:
                y = epi(y, rv, cv, mv)
            y = y.astype(out_dtype)
            if split_out:
                for sp in range(tn // 128):
                    o_ref[t * (tn // 128) + sp, 0] = y[:, sp * 128:(sp + 1) * 128]
            else:
                o_ref[0, :, t * tn:(t + 1) * tn] = y

    in_specs = [pl.BlockSpec((1, RB, wd), functools.partial(lambda b, i, cb: (b, i, cb), cb=cb))
                for (_, wd, cb) in rows]
    args = [a for (a, _, _) in rows]
    if mod is not None:
        in_specs.append(pl.BlockSpec((1, 1, 6, D), lambda b, i: (b, i, 0, 0)))
        args.append(mod)
    for cst in consts:
        in_specs.append(pl.BlockSpec(cst.shape, lambda b, i: (0, 0)))
        args.append(cst)
    in_specs.append(pl.BlockSpec((K, N), lambda b, i: (0, 0)))
    args.append(w)
    if split_out:
        out_specs = pl.BlockSpec((N // 128, 1, RB, 128), lambda b, i: (0, b, i, 0))
        out_shape = jax.ShapeDtypeStruct((N // 128, B, NT, 128), out_dtype)
    else:
        out_specs = pl.BlockSpec((1, RB, N), lambda b, i: (b, i, 0))
        out_shape = jax.ShapeDtypeStruct((B, NT, N), out_dtype)
    return pl.pallas_call(
        kern,
        grid=(B, NT // RB),
        in_specs=in_specs,
        out_specs=out_specs,
        out_shape=out_shape,
        compiler_params=_cp(("parallel", "parallel")),
        name=name,
    )(*args)


def _pro_modulate(rv, cv, mv, *, shift, scale):
    return rv[0] * (1.0 + mv(scale)) + mv(shift)


def _inproj(h, mod, w, *, out_dtype=F32, ffn=False, split_out=False, name):
    pro = functools.partial(_pro_modulate, shift=3 if ffn else 0, scale=4 if ffn else 1)
    return _rowmm([(h, D, 0)], [], w, pro, out_dtype=out_dtype, mod=mod, split_out=split_out, name=name)


def _head_norm(x, nheads, hd):
    outs = []
    for hh in range(nheads):
        xh = x[:, hh * hd:(hh + 1) * hd]
        mu = jnp.mean(xh, axis=-1, keepdims=True)
        dd = xh - mu
        var = jnp.mean(dd * dd, axis=-1, keepdims=True)
        outs.append(dd * lax.rsqrt(var + LN_EPS))
    return jnp.concatenate(outs, axis=-1)


def _pro_first(rv, cv, mv):
    return rv[0]


def _epi_ln(y, rv, cv, mv):
    return _layer_norm(ALPHA * rv[-1] + mv(2) * y, cv[-2], cv[-1])


def _conv_silu(src, ctile0, nct, w, b, nctx):
    B, NT, _ = src.shape
    CW = 256
    K = w.shape[0]

    def kern(x_ref, w_ref, b_ref, o_ref, scr):
        z8 = jnp.zeros((8, CW), F32)
        scr[0:8, :] = z8
        scr[8 + nctx:16 + nctx, :] = z8
        scr[16 + NT:24 + NT, :] = z8
        scr[8:8 + nctx, :] = x_ref[0, 0:nctx, :]
        scr[16 + nctx:16 + NT, :] = x_ref[0, nctx:NT, :]
        for base, n, out0 in ((8, nctx, 0), (16 + nctx, NT - nctx, nctx)):
            acc = jnp.zeros((n, CW), F32)
            for k in range(K):
                lo = base + k - K // 2
                acc = acc + w_ref[k:k + 1, :] * scr[lo:lo + n, :]
            o_ref[0, out0:out0 + n, :] = _silu(acc + b_ref[...])

    return pl.pallas_call(
        kern,
        grid=(B, nct),
        in_specs=[pl.BlockSpec((1, NT, CW), lambda bb, c: (bb, 0, ctile0 + c)),
                  pl.BlockSpec((K, CW), lambda bb, c: (0, c)),
                  pl.BlockSpec((1, CW), lambda bb, c: (0, c))],
        out_specs=pl.BlockSpec((1, NT, CW), lambda bb, c: (bb, 0, c)),
        out_shape=jax.ShapeDtypeStruct((B, NT, nct * CW), F32),
        scratch_shapes=[pltpu.VMEM((NT + 24, CW), F32)],
        compiler_params=_cp(("parallel", "parallel")),
        name="conv_silu",
    )(src, w, b.reshape(1, -1))


def _chunk_index(d, c, ncc, nc):
    bwd = jnp.where(c < ncc, ncc - 1 - c, nc - 1 - (c - ncc))
    return jnp.where(d == 0, c, bwd)


def _tri_masks():
    r = jnp.arange(CH)
    lower = (r[None, :] <= r[:, None]).astype(F32)
    return jnp.stack([lower, lower.T])


def _mlstm_scan(qk, v, gates, tri, nctx):
    B, NT, _ = v.shape
    nc, ncc = NT // CH, nctx // CH
    HD = A_HD
    scale = HD ** -0.5

    def kern(q_ref, k_ref, v_ref, g_ref, tri_ref, o_ref, ct_ref, n_ref, m_ref):
        @pl.when(pl.program_id(2) == 0)
        def _():
            ct_ref[...] = jnp.zeros(ct_ref.shape, F32)
            n_ref[...] = jnp.zeros(n_ref.shape, F32)
            m_ref[...] = jnp.zeros(m_ref.shape, F32)

        G = g_ref[0]
        tri_m = tri_ref[0]
        mask = tri_m > 0.5
        lane = lax.broadcasted_iota(jnp.int32, (CH, 128), 1)
        Glf = jnp.where((lane >= A_HEADS) & (lane < 2 * A_HEADS), _log_sigmoid(G), 0.0)
        cum_c = _dot_hi(tri_m, Glf)
        GT = G.T
        cum_r = _dot_nt_hi(Glf.T, tri_m)
        tot = jnp.sum(Glf, axis=0, keepdims=True)
        for hh in range(A_HEADS):
            sl = slice(hh * HD, (hh + 1) * HD)
            q = q_ref[0, :, sl] * scale
            k = k_ref[0, :, sl]
            vv = v_ref[0, :, sl]
            m = jnp.max(m_ref[hh:hh + 1, :], axis=1, keepdims=True)
            ig_c = _col(G, hh)
            b_c = _col(cum_c, A_HEADS + hh)
            ig_r = _row(GT, hh)
            b_r = _row(cum_r, A_HEADS + hh)
            logw = jnp.where(mask, b_c - b_r + ig_r, NEG)
            inter = b_c + m
            m_t = jnp.maximum(inter, jnp.max(logw, axis=1, keepdims=True))
            s = _dot_nt(q, k) * jnp.exp(logw - m_t)
            a = jnp.exp(inter - m_t)
            num = _dot(s, vv) + a * _dot(q, ct_ref[hh])
            den = jnp.sum(s, axis=1, keepdims=True) + a * jnp.sum(q * n_ref[hh:hh + 1, :], axis=1, keepdims=True)
            o_ref[0, :, sl] = num / jnp.maximum(jnp.abs(den), jnp.exp(-m_t))
            b_last = _col(tot, A_HEADS + hh)
            g_r = b_last - b_r + ig_r
            g_c = b_last - b_c + ig_c
            m_new = jnp.maximum(b_last + m, jnp.max(g_r, axis=1, keepdims=True))
            decay = jnp.exp(b_last + m - m_new)
            kw = k * jnp.exp(g_c - m_new)
            ct_ref[hh] = decay * ct_ref[hh] + _dot_tn(kw, vv)
            n_ref[hh:hh + 1, :] = decay * n_ref[hh:hh + 1, :] + jnp.sum(kw, axis=0, keepdims=True)
            m_ref[hh:hh + 1, :] = jnp.broadcast_to(m_new, (1, 128))

    ci = functools.partial(_chunk_index, ncc=ncc, nc=nc)
    W = A_HEADS * HD
    return pl.pallas_call(
        kern,
        grid=(B, 2, nc),
        in_specs=[pl.BlockSpec((1, CH, W), lambda b, d, c: (b, ci(d, c), 0)),
                  pl.BlockSpec((1, CH, W), lambda b, d, c: (b, ci(d, c), 1)),
                  pl.BlockSpec((1, CH, W), lambda b, d, c: (b, ci(d, c), 0)),
                  pl.BlockSpec((1, CH, 128), lambda b, d, c: (b, ci(d, c), d)),
                  pl.BlockSpec((1, CH, CH), lambda b, d, c: (d, 0, 0))],
        out_specs=pl.BlockSpec((1, CH, W), lambda b, d, c: (b, ci(d, c), d)),
        out_shape=jax.ShapeDtypeStruct((B, NT, 2 * W), F32),
        scratch_shapes=[pltpu.VMEM((A_HEADS, HD, HD), F32), pltpu.VMEM((8, HD), F32), pltpu.VMEM((8, 128), F32)],
        compiler_params=_cp(("parallel", "arbitrary", "arbitrary")),
        name="mlstm_scan",
    )(qk, qk, v, gates, tri)


def _rope_blocks(x, cos, sin, half):
    per = cos.shape[1] // 128
    lane = lax.broadcasted_iota(jnp.int32, (x.shape[0], 128), 1)
    outs = []
    for c in range(x.shape[1] // 128):
        xb = x[:, c * 128:(c + 1) * 128]
        cb = cos[:, (c % per) * 128:(c % per + 1) * 128]
        sb = sin[:, (c % per) * 128:(c % per + 1) * 128]
        if half == 64:
            partner = pltpu.roll(xb, 64, 1)
        else:
            lo = (lane % (2 * half)) < half
            partner = jnp.where(lo, pltpu.roll(xb, 128 - half, 1), pltpu.roll(xb, half, 1))
        outs.append(xb * cb + partner * sb)
    return jnp.concatenate(outs, axis=-1)


def _retention_scan(proj, cos, sin, dec, qd, kd, cd, nctx):
    B, NT, _ = proj.shape
    nc, ncc = NT // CH, nctx // CH
    scale = D_QK ** -0.5

    def kern(q_ref, k_ref, v_ref, cos_ref, sin_ref, dec_ref, qd_ref, kd_ref, cd_ref, o_ref, s_ref):
        @pl.when(pl.program_id(2) == 0)
        def _():
            s_ref[...] = jnp.zeros(s_ref.shape, F32)

        cos_t, sin_t = cos_ref[...], sin_ref[...]
        qd_t, kd_t = qd_ref[0], kd_ref[0]
        cd_t = cd_ref[0, 0:1, :]
        for hh in range(D_HEADS):
            qh = _rope_blocks(q_ref[0, :, hh * D_QK:(hh + 1) * D_QK], cos_t, sin_t, 64) * scale
            kh = _rope_blocks(k_ref[0, :, hh * D_QK:(hh + 1) * D_QK], cos_t, sin_t, 64)
            vh = v_ref[0, :, hh * D_V:(hh + 1) * D_V]
            s = _dot_nt(qh, kh) * dec_ref[0, hh]
            o_ref[0, :, hh * D_V:(hh + 1) * D_V] = _dot(s, vh) + _dot(qh, s_ref[hh]) * _col(qd_t, hh)
            s_ref[hh] = _col(cd_t, hh) * s_ref[hh] + _dot_tn(kh * _col(kd_t, hh), vh)

    ci = functools.partial(_chunk_index, ncc=ncc, nc=nc)
    WV = D_HEADS * D_V
    return pl.pallas_call(
        kern,
        grid=(B, 2, nc),
        in_specs=[pl.BlockSpec((1, CH, D), lambda b, d, c: (b, ci(d, c), 0)),
                  pl.BlockSpec((1, CH, D), lambda b, d, c: (b, ci(d, c), 1)),
                  pl.BlockSpec((1, CH, WV), lambda b, d, c: (b, ci(d, c), 1)),
                  pl.BlockSpec((CH, 256), lambda b, d, c: (ci(d, c), 0)),
                  pl.BlockSpec((CH, 256), lambda b, d, c: (ci(d, c), 0)),
                  pl.BlockSpec((1, D_HEADS, CH, CH), lambda b, d, c: (d, 0, 0, 0)),
                  pl.BlockSpec((1, CH, 128), lambda b, d, c: (d, 0, 0)),
                  pl.BlockSpec((1, CH, 128), lambda b, d, c: (d, 0, 0)),
                  pl.BlockSpec((1, 8, 128), lambda b, d, c: (d, 0, 0))],
        out_specs=pl.BlockSpec((1, CH, WV), lambda b, d, c: (b, ci(d, c), d)),
        out_shape=jax.ShapeDtypeStruct((B, NT, 2 * WV), F32),
        scratch_shapes=[pltpu.VMEM((D_HEADS, D_QK, D_V), F32)],
        compiler_params=_cp(("parallel", "arbitrary", "arbitrary")),
        name="retention_scan",
    )(proj, proj, proj, cos, sin, dec, qd, kd, cd)


def _ssd_scan(xbc, dtraw, dt_bias, a_rate, expand, tri, nctx):
    B, NT, _ = xbc.shape
    nc, ncc = NT // CH, nctx // CH
    R = B_HEADS // B_GROUPS
    GW = R * B_HD
    DI = B_HEADS * B_HD

    def kern(x_ref, b_ref, c_ref, dt_ref, bias_ref, a_ref, e_ref, tri_ref, o_ref, st_ref):
        @pl.when(pl.program_id(2) == 0)
        def _():
            st_ref[...] = jnp.zeros(st_ref.shape, F32)

        tri_m = tri_ref[0]
        mask = tri_m > 0.5
        E = e_ref[...]
        dt = _softplus(dt_ref[0] + bias_ref[0])
        dta = dt * a_ref[0]
        cum_c = _dot_hi(tri_m, dta)
        cum_r = _dot_nt_hi(dta.T, tri_m)
        a_last = jnp.sum(dta, axis=0, keepdims=True)
        xdt = x_ref[0] * _expand(dt, E)
        eacs = _expand(jnp.exp(cum_c), E)
        wst = _expand(jnp.exp(a_last - cum_c), E)
        da = _expand(jnp.broadcast_to(jnp.exp(a_last), (8, 128)), E)
        lane = lax.broadcasted_iota(jnp.int32, (CH, 128), 1)
        lo = lane < B_HD
        for g in range(B_GROUPS):
            bg = b_ref[0, :, g * B_STATE:(g + 1) * B_STATE]
            cg = c_ref[0, :, g * B_STATE:(g + 1) * B_STATE]
            cb = _dot_nt(cg, bg)
            xg = xdt[:, g * GW:(g + 1) * GW]
            y2 = _dot(cg, st_ref[g]) * eacs[:, g * GW:(g + 1) * GW]
            for pr in range(R // 2):
                xp = xg[:, pr * 128:(pr + 1) * 128]
                acc = y2[:, pr * 128:(pr + 1) * 128]
                for u in range(2):
                    hd = g * R + pr * 2 + u
                    seg = jnp.exp(jnp.where(mask, _col(cum_c, hd) - _row(cum_r, hd), NEG))
                    xm = jnp.where(lo if u == 0 else jnp.logical_not(lo), xp, 0.0)
                    acc = acc + _dot(cb * seg, xm)
                o_ref[0, :, g * GW + pr * 128:g * GW + (pr + 1) * 128] = acc
            st_ref[g] = da[0:1, g * GW:(g + 1) * GW] * st_ref[g] + _dot_tn(bg, xg * wst[:, g * GW:(g + 1) * GW])

    ci = functools.partial(_chunk_index, ncc=ncc, nc=nc)
    GS = B_GROUPS * B_STATE
    return pl.pallas_call(
        kern,
        grid=(B, 2, nc),
        in_specs=[pl.BlockSpec((1, CH, DI), lambda b, d, c: (b, ci(d, c), 0)),
                  pl.BlockSpec((1, CH, GS), lambda b, d, c: (b, ci(d, c), 2)),
                  pl.BlockSpec((1, CH, GS), lambda b, d, c: (b, ci(d, c), 3)),
                  pl.BlockSpec((1, CH, 128), lambda b, d, c: (b, ci(d, c), d)),
                  pl.BlockSpec((1, 1, 128), lambda b, d, c: (d, 0, 0)),
                  pl.BlockSpec((1, 1, 128), lambda b, d, c: (d, 0, 0)),
                  pl.BlockSpec((128, DI), lambda b, d, c: (0, 0)),
                  pl.BlockSpec((1, CH, CH), lambda b, d, c: (d, 0, 0))],
        out_specs=pl.BlockSpec((1, CH, DI), lambda b, d, c: (b, ci(d, c), d)),
        out_shape=jax.ShapeDtypeStruct((B, NT, 2 * DI), F32),
        scratch_shapes=[pltpu.VMEM((B_GROUPS, B_STATE, GW), F32)],
        compiler_params=_cp(("parallel", "arbitrary", "arbitrary")),
        name="ssd_scan",
    )(xbc, xbc, xbc, dtraw, dt_bias, a_rate, expand, tri)


def _attn_prep(qkv, cos, sin):
    B, NT, _ = qkv.shape
    scale = C_HD ** -0.5 * math.log2(math.e)

    def kern(x_ref, cos_ref, sin_ref, o_ref):
        j = pl.program_id(2)
        x = x_ref[0]

        @pl.when(j < 2)
        def _():
            y = _rope_blocks(x, cos_ref[...], sin_ref[...], C_HD // 4)
            o_ref[0] = (y * jnp.where(j == 0, scale, 1.0)).astype(BF)

        @pl.when(j == 2)
        def _():
            o_ref[0] = x.astype(BF)

    return pl.pallas_call(
        kern,
        grid=(B, NT // RB, 3),
        in_specs=[pl.BlockSpec((1, RB, D), lambda b, i, j: (b, i, j)),
                  pl.BlockSpec((RB, 128), lambda b, i, j: (i, 0)),
                  pl.BlockSpec((RB, 128), lambda b, i, j: (i, 0))],
        out_specs=pl.BlockSpec((1, RB, D), lambda b, i, j: (b, i, j)),
        out_shape=jax.ShapeDtypeStruct((B, NT, 3 * D), BF),
        compiler_params=_cp(("parallel", "parallel", "arbitrary")),
        name="attn_prep",
    )(qkv, cos, sin)


def _diff_attention(qkvb, lam, norm_g, out_scale, nctx):
    B, NT, _ = qkvb.shape
    TQ = RB
    HP = 2 * C_HD

    def kern(q_ref, k_ref, v_ref, lam_ref, g_ref, o_ref):
        lam_v = lam_ref[0:1, 0:1]
        lane = lax.broadcasted_iota(jnp.int32, (TQ, HP), 1)
        lo = lane < C_HD

        def attend(nk):
            for hp in range(C_HEADS):
                qp = q_ref[0, :, hp * HP:(hp + 1) * HP]
                kp = k_ref[0, 0:nk, hp * HP:(hp + 1) * HP]
                vp = v_ref[0, 0:nk, hp * HP:(hp + 1) * HP]
                outs = []
                for u in range(2):
                    qm = jnp.where(lo if u == 0 else jnp.logical_not(lo), qp, jnp.zeros_like(qp))
                    s = lax.dot_general(qm, kp, (((1,), (1,)), ((), ())), preferred_element_type=F32)
                    p = jnp.exp2(s - jnp.max(s, axis=1, keepdims=True))
                    z = jnp.sum(p, axis=1, keepdims=True)
                    outs.append(jnp.dot(p.astype(BF), vp, preferred_element_type=F32) / z)
                o = outs[0] - lam_v * outs[1]
                o = o * lax.rsqrt(jnp.mean(o * o, axis=-1, keepdims=True) + RMS_EPS)
                o_ref[0, :, hp * HP:(hp + 1) * HP] = o * g_ref[...] * out_scale

        is_ctx = pl.program_id(1) * TQ < nctx

        @pl.when(is_ctx)
        def _():
            attend(nctx)

        @pl.when(jnp.logical_not(is_ctx))
        def _():
            attend(NT)

    return pl.pallas_call(
        kern,
        grid=(B, NT // TQ),
        in_specs=[pl.BlockSpec((1, TQ, D), lambda b, i: (b, i, 0)),
                  pl.BlockSpec((1, NT, D), lambda b, i: (b, 0, 1)),
                  pl.BlockSpec((1, NT, D), lambda b, i: (b, 0, 2)),
                  pl.BlockSpec((8, 128), lambda b, i: (0, 0)),
                  pl.BlockSpec((1, HP), lambda b, i: (0, 0))],
        out_specs=pl.BlockSpec((1, TQ, D), lambda b, i: (b, i, 0)),
        out_shape=jax.ShapeDtypeStruct((B, NT, D), F32),
        compiler_params=_cp(("parallel", "arbitrary")),
        name="diff_attention",
    )(qkvb, qkvb, qkvb, lam, norm_g.reshape(1, HP))


def _sort16_network():
    n, pairs, p = 16, [], 1
    while p < n:
        k = p
        while k >= 1:
            for j in range(k % p, n - k, 2 * k):
                for i in range(min(k, n - j - k)):
                    if (i + j) // (2 * p) == (i + j + k) // (2 * p):
                        pairs.append((i + j, i + j + k))
            k //= 2
        p *= 2
    return pairs


def _top16_desc(s):
    v = [s[8 * k:8 * (k + 1), :] for k in range(16)]
    for a, b in _sort16_network():
        v[a], v[b] = jnp.maximum(v[a], v[b]), jnp.minimum(v[a], v[b])
    tops = []
    for r in range(P_TOPK):
        m = jnp.max(v[0], axis=0, keepdims=True)
        tops.append(m)
        if r + 1 < P_TOPK:
            hit = v[0] == m
            for k in range(P_TOPK - 1 - r):
                v[k] = jnp.where(hit, v[k + 1], v[k])
    return tops


def _peer_ln(h, q16, mod, keys, u, vt, ln_g, ln_b, ts):
    B, NT, _ = h.shape
    TS, TE = ts, P_TE
    assert NT % TS == 0 and TS % RB == 0
    E = u.shape[0]
    ne = E // TE
    NTB = TS // 128
    NIB = TE // P_NK
    RPB = TS // RB
    hf = h.reshape(B * NT, D)
    modf = mod.reshape(B * (NT // RB), 6, D)
    pairs = [(ra, rb) for ra in range(P_TOPK) for rb in range(P_TOPK // (ra + 1))]
    NCAND = 8 * ((len(pairs) + 7) // 8)

    def kern(h_ref, q_ref, mod_ref, keys_ref, u_ref, vt_ref, g_ref, b_ref, o_ref,
             xin, rbk, e2m, lr, f1, cand, a_sc, w_sc, acc):
        e = pl.program_id(1)

        @pl.when(e == 0)
        def _():
            for r in range(RPB):
                hx = h_ref[r * RB:(r + 1) * RB, :]
                xin[:, r * RB:(r + 1) * RB] = (hx * (1.0 + mod_ref[r, 4:5, :]) + mod_ref[r, 3:4, :]).T.astype(BF)
            acc[...] = jnp.zeros(acc.shape, F32)

            def head_body(hh, carry):
                for tb in range(NTB):
                    tsl = slice(tb * 128, (tb + 1) * 128)
                    vals = []
                    for c in range(2):
                        s = lax.dot_general(keys_ref[2 * hh + c], q_ref[2 * hh + c, tsl, :],
                                            (((1,), (1,)), ((), ())), preferred_element_type=F32)
                        vals.append((s, _top16_desc(s)))
                    (s1, ta), (s2, tbv) = vals
                    cand[...] = jnp.full(cand.shape, NEG, F32)
                    for n, (pa, pb) in enumerate(pairs):
                        cand[n:n + 1, :] = ta[pa] + tbv[pb]
                    cv = cand[...]
                    work = cv
                    for r in range(P_TOPK):
                        th = jnp.max(work, axis=0, keepdims=True)
                        if r + 1 < P_TOPK:
                            work = jnp.where(work == th, NEG, work)
                    top = ta[0] + tbv[0]
                    z = jnp.sum(jnp.where(cv >= th, jnp.exp(cv - top), 0.0), axis=0, keepdims=True)
                    lrv = jnp.zeros(s1.shape, F32)
                    rb = jnp.zeros(s2.shape, F32)
                    for pa in range(P_TOPK):
                        cnt = jnp.zeros(th.shape, F32)
                        for pb in range(P_TOPK // (pa + 1)):
                            cnt = cnt + jnp.where(ta[pa] + tbv[pb] >= th, 1.0, 0.0)
                        lrv = jnp.where(s1 == ta[pa], cnt, lrv)
                        rb = jnp.where(s2 < tbv[pa], float(pa + 1), rb)
                    lr[hh, :, tsl] = lrv
                    f1[hh, :, tsl] = jnp.where(s1 >= ta[-1], jnp.exp(s1 - ta[0]), 0.0)
                    rbk[hh, :, tsl] = rb.astype(BF)
                    e2m[hh, :, tsl] = (jnp.where(s2 >= tbv[-1], jnp.exp(s2 - tbv[0]), 0.0) / z).astype(BF)
                return carry

            lax.fori_loop(0, P_HEADS, head_body, 0)

        a_sc[...] = jnp.dot(u_ref[...], xin[...], preferred_element_type=F32)
        i8 = pl.multiple_of(e * NIB, 8)
        zero = jnp.zeros((), BF)
        for tb in range(NTB):
            tsl = slice(tb * 128, (tb + 1) * 128)
            lrb = [lr[hh, pl.ds(i8, NIB), tsl] for hh in range(P_HEADS)]
            f1b = [f1[hh, pl.ds(i8, NIB), tsl] for hh in range(P_HEADS)]
            for k in range(NIB):
                gm = jnp.zeros((P_NK, 128), BF)
                for hh in range(P_HEADS):
                    sel = jnp.where(rbk[hh, :, tsl] < lrb[hh][k:k + 1, :].astype(BF), e2m[hh, :, tsl], zero)
                    gm = gm + sel * f1b[hh][k:k + 1, :].astype(BF)
                ksl = slice(k * P_NK, (k + 1) * P_NK)
                w_sc[ksl, tsl] = gm * _gelu(a_sc[ksl, tsl].astype(BF))
        acc[...] += jnp.dot(vt_ref[0], w_sc[...], preferred_element_type=F32)

        @pl.when(e == ne - 1)
        def _():
            y = acc[...].T
            for r in range(RPB):
                hx = h_ref[r * RB:(r + 1) * RB, :]
                o_ref[r * RB:(r + 1) * RB, :] = _layer_norm(
                    ALPHA * hx + mod_ref[r, 5:6, :] * y[r * RB:(r + 1) * RB, :], g_ref[...], b_ref[...])

    out = pl.pallas_call(
        kern,
        grid=(B * NT // TS, ne),
        in_specs=[pl.BlockSpec((TS, D), lambda s, e: (s, 0)),
                  pl.BlockSpec((2 * P_HEADS, TS, P_NK), lambda s, e: (0, s, 0)),
                  pl.BlockSpec((RPB, 6, D), lambda s, e: (s, 0, 0)),
                  pl.BlockSpec((2 * P_HEADS, P_NK, P_NK), lambda s, e: (0, 0, 0)),
                  pl.BlockSpec((TE, D), lambda s, e: (e, 0)),
                  pl.BlockSpec((1, D, TE), lambda s, e: (e, 0, 0)),
                  pl.BlockSpec((1, D), lambda s, e: (0, 0)),
                  pl.BlockSpec((1, D), lambda s, e: (0, 0))],
        out_specs=pl.BlockSpec((TS, D), lambda s, e: (s, 0)),
        out_shape=jax.ShapeDtypeStruct((B * NT, D), F32),
        scratch_shapes=[pltpu.VMEM((D, TS), BF),
                        pltpu.VMEM((P_HEADS, P_NK, TS), BF), pltpu.VMEM((P_HEADS, P_NK, TS), BF),
                        pltpu.VMEM((P_HEADS, P_NK, TS), F32), pltpu.VMEM((P_HEADS, P_NK, TS), F32),
                        pltpu.VMEM((NCAND, 128), F32),
                        pltpu.VMEM((TE, TS), F32), pltpu.VMEM((TE, TS), BF),
                        pltpu.VMEM((D, TS), F32)],
        compiler_params=_cp(("parallel", "arbitrary")),
        name="peer",
    )(hf, q16, modf, keys, u, vt, ln_g.reshape(1, D), ln_b.reshape(1, D))
    return out.reshape(B, NT, D)


def _rope_table(nctx, nlat, half, reps):
    t = jnp.arange(nlat, dtype=jnp.int32)
    freqs = ROPE_BASE ** (-jnp.arange(half, dtype=F32) / half)
    sign = jnp.concatenate([-jnp.ones((half,), F32), jnp.ones((half,), F32)])
    cs, sn = [], []
    for pos in (t // GRID_W, t % GRID_W):
        ang = pos.astype(F32)[:, None] * freqs
        cs.append(jnp.concatenate([jnp.cos(ang), jnp.cos(ang)], axis=-1))
        sn.append(jnp.concatenate([jnp.sin(ang), jnp.sin(ang)], axis=-1) * sign)
    cos = jnp.tile(jnp.concatenate(cs, axis=-1), (1, reps))
    sin = jnp.tile(jnp.concatenate(sn, axis=-1), (1, reps))
    cos = jnp.concatenate([jnp.ones((nctx, cos.shape[1]), F32), cos], axis=0)
    sin = jnp.concatenate([jnp.zeros((nctx, sin.shape[1]), F32), sin], axis=0)
    return cos, sin


def _retention_tables(decay_logit):
    lg = jax.nn.log_sigmoid(decay_logit.astype(F32))
    pos = jnp.arange(CH, dtype=F32)
    rel = pos[:, None] - pos[None, :]
    lower = jnp.where(rel >= 0, jnp.exp(jnp.maximum(rel, 0.0) * lg[:, :, None, None]), 0.0)
    dec = jnp.stack([lower[0], jnp.swapaxes(lower[1], -1, -2)])
    qd_f = jnp.exp((pos + 1.0)[:, None] * lg[0][None, :])
    kd_f = jnp.exp((CH - 1.0 - pos)[:, None] * lg[0][None, :])
    qd_b = jnp.exp((CH - pos)[:, None] * lg[1][None, :])
    kd_b = jnp.exp(pos[:, None] * lg[1][None, :])
    pad = lambda a: jnp.pad(a, ((0, 0), (0, 128 - a.shape[1])))
    qd = jnp.stack([pad(qd_f), pad(qd_b)])
    kd = jnp.stack([pad(kd_f), pad(kd_b)])
    cd = jnp.broadcast_to(jnp.pad(jnp.exp(CH * lg), ((0, 0), (0, 128 - D_HEADS)))[:, None, :], (2, 8, 128))
    return dec, qd, kd, cd


def _mlstm_layer(h, mod, lnc, tri, nctx, w_up, conv_w, conv_b, w_qk, w_v, w_gate, b_gate, norm_g, skip, w_down):
    DI = A_HEADS * A_HD
    up = _inproj(h, mod, w_up.astype(BF), name="mlstm_up")
    xc = _conv_silu(up, 0, DI // 256, conv_w, conv_b, nctx)
    qk = _rowmm([(xc, DI, 0)], [], w_qk.astype(BF), _pro_first, out_dtype=F32, name="mlstm_qk")
    v = _rowmm([(up, DI, 0)], [], w_v.astype(BF), _pro_first, out_dtype=F32, name="mlstm_v")
    wg = w_gate.reshape(3 * DI, 4, A_HEADS)
    bg = b_gate.reshape(4, A_HEADS)
    zw = jnp.zeros((3 * DI, 128 - 2 * A_HEADS), F32)
    zb = jnp.zeros((128 - 2 * A_HEADS,), F32)
    wg2 = jnp.concatenate([wg[:, 0], wg[:, 1], zw, wg[:, 2], wg[:, 3], zw], axis=1).astype(BF)
    bg2 = jnp.concatenate([bg[0], bg[1], zb, bg[2], bg[3], zb]).reshape(1, 256)

    def gate_epi(y, rv, cv, mv):
        return y + _dot(rv[1], cv[0]) + _dot(rv[2], cv[1]) + cv[2]

    gates = _rowmm([(v, DI, 0), (qk, DI, 0), (qk, DI, 1)], [wg2[:DI], wg2[DI:2 * DI], bg2], wg2[2 * DI:],
                   _pro_first, gate_epi, out_dtype=F32, name="mlstm_gates")
    hs = _mlstm_scan(qk, v, gates, tri, nctx)

    def out_pro(rv, cv, mv):
        hf, hb, z, o_pre, xcv, _ = rv
        hh = _sigmoid(o_pre) * (hf + hb)
        return (_head_norm(hh, A_HEADS, A_HD) * cv[0] + cv[1] * xcv) * _silu(z)

    return _rowmm([(hs, DI, 0), (hs, DI, 1), (up, DI, 1), (up, DI, 2), (xc, DI, 0), (h, D, 0)],
                  [norm_g.reshape(1, DI), skip.reshape(1, DI)] + lnc, w_down.astype(BF), out_pro, _epi_ln,
                  out_dtype=F32, mod=mod, name="mlstm_out")


def _ssd_layer(h, mod, lnc, tri, nctx, w_in, conv_w, conv_b, dt_bias, a_log, d_skip, norm_g, w_out):
    DI = B_HEADS * B_HD
    CC = DI + 2 * B_GROUPS * B_STATE
    proj = _inproj(h, mod, w_in[:, :DI + CC].astype(BF), name="ssd_in")
    wdt = w_in[:, DI + CC:].reshape(D, 2, B_HEADS)
    zpad = jnp.zeros((D, 128 - B_HEADS), F32)
    wdt2 = jnp.concatenate([wdt[:, 0], zpad, wdt[:, 1], zpad], axis=1).astype(BF)
    dtraw = _inproj(h, mod, wdt2, name="ssd_dt")
    xbc = _conv_silu(proj, DI // 256, CC // 256, conv_w, conv_b, nctx)
    padl = lambda a: jnp.pad(a.astype(F32), ((0, 0), (0, 128 - B_HEADS))).reshape(2, 1, 128)
    expand = (jnp.arange(128)[:, None] == (jnp.arange(DI)[None, :] // B_HD)).astype(BF)
    ys = _ssd_scan(xbc, dtraw, padl(dt_bias), padl(-jnp.exp(a_log.astype(F32))), expand, tri, nctx)
    dvec = jnp.repeat(d_skip.astype(F32), B_HD).reshape(1, DI)
    GW = DI // B_GROUPS

    def out_pro(rv, cv, mv):
        yf, yb, xs, z, _ = rv
        y = (yf + yb + cv[0] * xs) * _silu(z)
        outs = []
        for g in range(B_GROUPS):
            yg = y[:, g * GW:(g + 1) * GW]
            outs.append(yg * lax.rsqrt(jnp.mean(yg * yg, axis=-1, keepdims=True) + RMS_EPS))
        return jnp.concatenate(outs, axis=-1) * cv[1]

    return _rowmm([(ys, DI, 0), (ys, DI, 1), (xbc, DI, 0), (proj, DI, 0), (h, D, 0)],
                  [dvec, norm_g.reshape(1, DI)] + lnc, w_out.astype(BF), out_pro, _epi_ln,
                  out_dtype=F32, mod=mod, name="ssd_out")


def _diff_layer(h, mod, lnc, nctx, layer_idx, w_qkv, lam_vecs, norm_g, w_out):
    B, NT, _ = h.shape
    lam_init = 0.8 - 0.6 * math.exp(-0.3 * layer_idx)
    lv = lam_vecs.astype(F32)
    lam = jnp.exp(jnp.sum(lv[0] * lv[1])) - jnp.exp(jnp.sum(lv[2] * lv[3])) + lam_init
    qkv = _inproj(h, mod, w_qkv.astype(BF), name="diff_qkv")
    cos, sin = _rope_table(nctx, NT - nctx, C_HD // 4, 2)
    qkvb = _attn_prep(qkv, cos, sin)
    o = _diff_attention(qkvb, jnp.full((8, 128), lam, F32), norm_g, 1.0 - lam_init, nctx)
    return _rowmm([(o, D, 0), (h, D, 0)], lnc, w_out.astype(BF), _pro_first, _epi_ln, out_dtype=F32, mod=mod,
                  name="diff_out")


def _retention_layer(h, mod, lnc, nctx, w_in, decay_logit, norm_g, w_out):
    B, NT, _ = h.shape
    DV = D_HEADS * D_V
    proj = _inproj(h, mod, w_in.astype(BF), name="ret_in")
    cos, sin = _rope_table(nctx, NT - nctx, D_QK // 4, 1)
    dec, qd, kd, cd = _retention_tables(decay_logit)
    os_ = _retention_scan(proj, cos, sin, dec, qd, kd, cd, nctx)

    def out_pro(rv, cv, mv):
        of, ob, g, _ = rv
        return _head_norm(of + ob, D_HEADS, D_V) * cv[0] * _silu(g)

    return _rowmm([(os_, DV, 0), (os_, DV, 1), (proj, DV, 2), (h, D, 0)], [norm_g.reshape(1, DV)] + lnc,
                  w_out.astype(BF), out_pro, _epi_ln, out_dtype=F32, mod=mod, name="ret_out")


def kernel(x, c, ctx, c_ctx, ada_w, ada_b, ln_g, ln_b, peer_wq, peer_keys, peer_u, peer_v, mlstm_w_up, mlstm_conv_w, mlstm_conv_b, mlstm_w_qk, mlstm_w_v, mlstm_w_gate, mlstm_b_gate, mlstm_norm_g, mlstm_skip, mlstm_w_down, ssd_w_in, ssd_conv_w, ssd_conv_b, ssd_dt_bias, ssd_a_log, ssd_d, ssd_norm_g, ssd_w_out, diff_w_qkv, diff_lambda, diff_norm_g, diff_w_out, ret_w_in, ret_decay_logit, ret_norm_g, ret_w_out):
    B, nlat, _ = x.shape
    nctx = ctx.shape[1]
    NT = nctx + nlat
    nrb = NT // RB
    h = jnp.concatenate([ctx, x], axis=1)
    rows = 8 * ((B + 1 + 7) // 8)
    cond = jnp.concatenate([c, c_ctx[None, :], jnp.zeros((rows - B - 1, D), F32)], axis=0)
    modall = _modulation(cond, ada_w, ada_b)
    tri = _tri_masks()
    for i in range(DEPTH):
        kind, j = i % 4, i // 4
        ml = modall[i, :B].reshape(B, 1, 6, D)
        mc = modall[i, B].reshape(1, 1, 6, D)
        mod = jnp.concatenate([jnp.broadcast_to(mc, (B, nctx // RB, 6, D)),
                               jnp.broadcast_to(ml, (B, nlat // RB, 6, D))], axis=1).reshape(B, nrb, 6, D)
        lnc = [ln_g[i, 0].reshape(1, D), ln_b[i, 0].reshape(1, D)]
        if kind == 0:
            h = _mlstm_layer(h, mod, lnc, tri, nctx, mlstm_w_up[j], mlstm_conv_w[j], mlstm_conv_b[j], mlstm_w_qk[j],
                             mlstm_w_v[j], mlstm_w_gate[j], mlstm_b_gate[j], mlstm_norm_g[j], mlstm_skip[j],
                             mlstm_w_down[j])
        elif kind == 1:
            h = _ssd_layer(h, mod, lnc, tri, nctx, ssd_w_in[j], ssd_conv_w[j], ssd_conv_b[j], ssd_dt_bias[j],
                           ssd_a_log[j], ssd_d[j], ssd_norm_g[j], ssd_w_out[j])
        elif kind == 2:
            h = _diff_layer(h, mod, lnc, nctx, i, diff_w_qkv[j], diff_lambda[j], diff_norm_g[j], diff_w_out[j])
        else:
            h = _retention_layer(h, mod, lnc, nctx, ret_w_in[j], ret_decay_logit[j], ret_norm_g[j], ret_w_out[j])
        ts = P_TS
        if i == DEPTH - 1:
            h, mod, ts = h[:, nctx:, :], mod[:, nctx // RB:], P_TS_LAST
        q16 = _inproj(h, mod, peer_wq[i].astype(BF), out_dtype=BF, ffn=True, split_out=True,
                      name="peer_q").reshape(2 * P_HEADS, -1, P_NK)
        keys = peer_keys[i].reshape(2 * P_HEADS, P_NK, P_NK).astype(BF)
        vt = jnp.swapaxes(peer_v[i].astype(BF).reshape(-1, P_TE, D), 1, 2)
        h = _peer_ln(h, q16, mod, keys, peer_u[i].astype(BF), vt, ln_g[i, 1], ln_b[i, 1], ts)
    return h
```

```python
import functools
import math

import jax
import jax.numpy as jnp
from jax import lax
from jax.experimental import pallas as pl
from jax.experimental.pallas import tpu as pltpu

F32 = jnp.float32
BF = jnp.bfloat16
HI = lax.Precision.HIGHEST

D = 1024
DEPTH = 4
GRID_W = 64
CH = 128
RB = 256
LN_EPS = 1e-5
RMS_EPS = 1e-6
ALPHA = (2.0 * DEPTH) ** 0.25
ROPE_BASE = 10000.0
NEG = -jnp.inf

A_HEADS, A_HD = 4, 512
B_HEADS, B_HD, B_GROUPS, B_STATE = 32, 64, 8, 128
C_HEADS, C_HD = 8, 64
D_HEADS, D_QK, D_V = 4, 256, 512
P_HEADS, P_NK, P_TOPK = 8, 128, 16
P_TS = 768
P_TS_LAST = 512
P_TE = 1024

VMEM_LIMIT = 56 * 1024 * 1024


def _cp(sem):
    return pltpu.CompilerParams(dimension_semantics=sem, vmem_limit_bytes=VMEM_LIMIT)


def _dot(a, b):
    return jnp.dot(a.astype(BF), b.astype(BF), preferred_element_type=F32)


def _dot_nt(a, b):
    return lax.dot_general(a.astype(BF), b.astype(BF), (((1,), (1,)), ((), ())),
                           preferred_element_type=F32)


def _dot_tn(a, b):
    return lax.dot_general(a.astype(BF), b.astype(BF), (((0,), (0,)), ((), ())),
                           preferred_element_type=F32)


def _dot_hi(a, b):
    return jnp.dot(a, b, preferred_element_type=F32, precision=HI)


def _dot_nt_hi(a, b):
    return lax.dot_general(a, b, (((1,), (1,)), ((), ())), preferred_element_type=F32, precision=HI)


def _expand(x, e01):
    hi = x.astype(BF)
    lo = (x - hi.astype(F32)).astype(BF)
    return (jnp.dot(hi, e01.astype(BF), preferred_element_type=F32)
            + jnp.dot(lo, e01.astype(BF), preferred_element_type=F32))


def _col(x, idx):
    lane = lax.broadcasted_iota(jnp.int32, x.shape, 1)
    return jnp.sum(jnp.where(lane == idx, x, 0.0), axis=1, keepdims=True)


def _row(x, idx):
    sub = lax.broadcasted_iota(jnp.int32, x.shape, 0)
    return jnp.sum(jnp.where(sub == idx, x, 0.0), axis=0, keepdims=True)


def _sigmoid(x):
    return 1.0 / (1.0 + jnp.exp(-x))


def _silu(x):
    return x * _sigmoid(x)


def _softplus(x):
    return jnp.maximum(x, 0.0) + jnp.log(1.0 + jnp.exp(-jnp.abs(x)))


def _log_sigmoid(x):
    return jnp.minimum(x, 0.0) - jnp.log(1.0 + jnp.exp(-jnp.abs(x)))


def _gelu_x2(x):
    c = math.sqrt(2.0 / math.pi)
    return x + x * jnp.tanh(x * (c + (0.044715 * c) * (x * x)))


def _layer_norm(r, g, b):
    mu = jnp.mean(r, axis=-1, keepdims=True)
    d = r - mu
    var = jnp.mean(d * d, axis=-1, keepdims=True)
    return d * lax.rsqrt(var + LN_EPS) * g + b


def _modulation(cond, ada_w, ada_b):
    R = cond.shape[0]

    def kern(c_ref, w_ref, b_ref, o_ref):
        o_ref[0] = _dot(_silu(c_ref[...]), w_ref[0]) + b_ref[0]

    return pl.pallas_call(
        kern,
        grid=(DEPTH, 6),
        in_specs=[pl.BlockSpec((R, D), lambda l, j: (0, 0)),
                  pl.BlockSpec((1, D, D), lambda l, j: (l, 0, j)),
                  pl.BlockSpec((1, 1, D), lambda l, j: (l, 0, j))],
        out_specs=pl.BlockSpec((1, R, D), lambda l, j: (l, 0, j)),
        out_shape=jax.ShapeDtypeStruct((DEPTH, R, 6 * D), F32),
        compiler_params=_cp(("parallel", "parallel")),
        name="modulation",
    )(cond, ada_w, ada_b.reshape(DEPTH, 1, 6 * D))


def _rowmm(rows, consts, w, pro, epi=None, *, out_dtype, mod=None, split_out=False, name):
    B, NT = rows[0][0].shape[:2]
    K, N = w.shape
    tn = min(N, 1024)
    assert N % tn == 0 and (epi is None or N == tn)
    nr, nc = len(rows), len(consts)

    def kern(*refs):
        rv = [r[0] for r in refs[:nr]]
        p = nr
        mv = None
        if mod is not None:
            mod_ref = refs[p]
            mv = lambda k: mod_ref[0, 0, k:k + 1, :]
            p += 1
        cv = [r[...] for r in refs[p:p + nc]]
        w_ref, o_ref = refs[p + nc], refs[p + nc + 1]
        xb = pro(rv, cv, mv).astype(BF)
        for t in range(N // tn):
            y = jnp.dot(xb, w_ref[:, t * tn:(t + 1) * tn], preferred_element_type=F32)
            if epi is not None:
                y = epi(y, rv, cv, mv)
            y = y.astype(out_dtype)
            if split_out:
                for sp in range(tn // 128):
                    o_ref[t * (tn // 128) + sp, 0] = y[:, sp * 128:(sp + 1) * 128]
            else:
                o_ref[0, :, t * tn:(t + 1) * tn] = y

    in_specs = [pl.BlockSpec((1, RB, wd), functools.partial(lambda b, i, cb: (b, i, cb), cb=cb))
                for (_, wd, cb) in rows]
    args = [a for (a, _, _) in rows]
    if mod is not None:
        in_specs.append(pl.BlockSpec((1, 1, 6, D), lambda b, i: (b, i, 0, 0)))
        args.append(mod)
    for cst in consts:
        in_specs.append(pl.BlockSpec(cst.shape, lambda b, i: (0, 0)))
        args.append(cst)
    in_specs.append(pl.BlockSpec((K, N), lambda b, i: (0, 0)))
    args.append(w)
    if split_out:
        out_specs = pl.BlockSpec((N // 128, 1, RB, 128), lambda b, i: (0, b, i, 0))
        out_shape = jax.ShapeDtypeStruct((N // 128, B, NT, 128), out_dtype)
    else:
        out_specs = pl.BlockSpec((1, RB, N), lambda b, i: (b, i, 0))
        out_shape = jax.ShapeDtypeStruct((B, NT, N), out_dtype)
    return pl.pallas_call(
        kern,
        grid=(B, NT // RB),
        in_specs=in_specs,
        out_specs=out_specs,
        out_shape=out_shape,
        compiler_params=_cp(("parallel", "parallel")),
        name=name,
    )(*args)


def _pro_modulate(rv, cv, mv, *, shift, scale):
    return rv[0] * (1.0 + mv(scale)) + mv(shift)


def _inproj(h, mod, w, *, out_dtype=F32, ffn=False, split_out=False, name):
    pro = functools.partial(_pro_modulate, shift=3 if ffn else 0, scale=4 if ffn else 1)
    return _rowmm([(h, D, 0)], [], w, pro, out_dtype=out_dtype, mod=mod, split_out=split_out, name=name)


def _head_norm(x, nheads, hd):
    outs = []
    for hh in range(nheads):
        xh = x[:, hh * hd:(hh + 1) * hd]
        mu = jnp.mean(xh, axis=-1, keepdims=True)
        dd = xh - mu
        var = jnp.mean(dd * dd, axis=-1, keepdims=True)
        outs.append(dd * lax.rsqrt(var + LN_EPS))
    return jnp.concatenate(outs, axis=-1)


def _pro_first(rv, cv, mv):
    return rv[0]


def _epi_ln(y, rv, cv, mv):
    return _layer_norm(ALPHA * rv[-1] + mv(2) * y, cv[-2], cv[-1])


def _conv_silu(src, ctile0, nct, w, b, nctx):
    B, NT, _ = src.shape
    CW = 256
    K = w.shape[0]

    def kern(x_ref, w_ref, b_ref, o_ref, scr):
        z8 = jnp.zeros((8, CW), F32)
        scr[0:8, :] = z8
        scr[8 + nctx:16 + nctx, :] = z8
        scr[16 + NT:24 + NT, :] = z8
        scr[8:8 + nctx, :] = x_ref[0, 0:nctx, :]
        scr[16 + nctx:16 + NT, :] = x_ref[0, nctx:NT, :]
        for base, n, out0 in ((8, nctx, 0), (16 + nctx, NT - nctx, nctx)):
            acc = jnp.zeros((n, CW), F32)
            for k in range(K):
                lo = base + k - K // 2
                acc = acc + w_ref[k:k + 1, :] * scr[lo:lo + n, :]
            o_ref[0, out0:out0 + n, :] = _silu(acc + b_ref[...])

    return pl.pallas_call(
        kern,
        grid=(B, nct),
        in_specs=[pl.BlockSpec((1, NT, CW), lambda bb, c: (bb, 0, ctile0 + c)),
                  pl.BlockSpec((K, CW), lambda bb, c: (0, c)),
                  pl.BlockSpec((1, CW), lambda bb, c: (0, c))],
        out_specs=pl.BlockSpec((1, NT, CW), lambda bb, c: (bb, 0, c)),
        out_shape=jax.ShapeDtypeStruct((B, NT, nct * CW), F32),
        scratch_shapes=[pltpu.VMEM((NT + 24, CW), F32)],
        compiler_params=_cp(("parallel", "parallel")),
        name="conv_silu",
    )(src, w, b.reshape(1, -1))


def _chunk_index(d, c, ncc, nc):
    bwd = jnp.where(c < ncc, ncc - 1 - c, nc - 1 - (c - ncc))
    return jnp.where(d == 0, c, bwd)


def _tri_masks():
    r = jnp.arange(CH)
    lower = (r[None, :] <= r[:, None]).astype(F32)
    return jnp.stack([lower, lower.T])


def _mlstm_scan(qk, v, gates, tri, nctx):
    B, NT, _ = v.shape
    nc, ncc = NT // CH, nctx // CH
    HD = A_HD
    scale = HD ** -0.5

    def kern(q_ref, k_ref, v_ref, g_ref, tri_ref, o_ref, ct_ref, n_ref, m_ref):
        @pl.when(pl.program_id(2) == 0)
        def _():
            ct_ref[...] = jnp.zeros(ct_ref.shape, F32)
            n_ref[...] = jnp.zeros(n_ref.shape, F32)
            m_ref[...] = jnp.zeros(m_ref.shape, F32)

        G = g_ref[0]
        tri_m = tri_ref[0]
        mask = tri_m > 0.5
        lane = lax.broadcasted_iota(jnp.int32, (CH, 128), 1)
        Glf = jnp.where((lane >= A_HEADS) & (lane < 2 * A_HEADS), _log_sigmoid(G), 0.0)
        cum_c = _dot_hi(tri_m, Glf)
        GT = G.T
        cum_r = _dot_nt_hi(Glf.T, tri_m)
        tot = jnp.sum(Glf, axis=0, keepdims=True)
        for hh in range(A_HEADS):
            sl = slice(hh * HD, (hh + 1) * HD)
            q = q_ref[0, :, sl] * scale
            k = k_ref[0, :, sl]
            vv = v_ref[0, :, sl]
            m = jnp.max(m_ref[hh:hh + 1, :], axis=1, keepdims=True)
            ig_c = _col(G, hh)
            b_c = _col(cum_c, A_HEADS + hh)
            ig_r = _row(GT, hh)
            b_r = _row(cum_r, A_HEADS + hh)
            logw = jnp.where(mask, b_c - b_r + ig_r, NEG)
            inter = b_c + m
            m_t = jnp.maximum(inter, jnp.max(logw, axis=1, keepdims=True))
            s = _dot_nt(q, k) * jnp.exp(logw - m_t)
            a = jnp.exp(inter - m_t)
            num = _dot(s, vv) + a * _dot(q, ct_ref[hh])
            den = jnp.sum(s, axis=1, keepdims=True) + a * jnp.sum(q * n_ref[hh:hh + 1, :], axis=1, keepdims=True)
            o_ref[0, :, sl] = num / jnp.maximum(jnp.abs(den), jnp.exp(-m_t))
            b_last = _col(tot, A_HEADS + hh)
            g_r = b_last - b_r + ig_r
            g_c = b_last - b_c + ig_c
            m_new = jnp.maximum(b_last + m, jnp.max(g_r, axis=1, keepdims=True))
            decay = jnp.exp(b_last + m - m_new)
            kw = k * jnp.exp(g_c - m_new)
            ct_ref[hh] = decay * ct_ref[hh] + _dot_tn(kw, vv)
            n_ref[hh:hh + 1, :] = decay * n_ref[hh:hh + 1, :] + jnp.sum(kw, axis=0, keepdims=True)
            m_ref[hh:hh + 1, :] = jnp.broadcast_to(m_new, (1, 128))

    ci = functools.partial(_chunk_index, ncc=ncc, nc=nc)
    W = A_HEADS * HD
    return pl.pallas_call(
        kern,
        grid=(B, 2, nc),
        in_specs=[pl.BlockSpec((1, CH, W), lambda b, d, c: (b, ci(d, c), 0)),
                  pl.BlockSpec((1, CH, W), lambda b, d, c: (b, ci(d, c), 1)),
                  pl.BlockSpec((1, CH, W), lambda b, d, c: (b, ci(d, c), 0)),
                  pl.BlockSpec((1, CH, 128), lambda b, d, c: (b, ci(d, c), d)),
                  pl.BlockSpec((1, CH, CH), lambda b, d, c: (d, 0, 0))],
        out_specs=pl.BlockSpec((1, CH, W), lambda b, d, c: (b, ci(d, c), d)),
        out_shape=jax.ShapeDtypeStruct((B, NT, 2 * W), F32),
        scratch_shapes=[pltpu.VMEM((A_HEADS, HD, HD), F32), pltpu.VMEM((8, HD), F32), pltpu.VMEM((8, 128), F32)],
        compiler_params=_cp(("parallel", "arbitrary", "arbitrary")),
        name="mlstm_scan",
    )(qk, qk, v, gates, tri)


def _rope_blocks(x, cos, sin, half):
    per = cos.shape[1] // 128
    lane = lax.broadcasted_iota(jnp.int32, (x.shape[0], 128), 1)
    outs = []
    for c in range(x.shape[1] // 128):
        xb = x[:, c * 128:(c + 1) * 128]
        cb = cos[:, (c % per) * 128:(c % per + 1) * 128]
        sb = sin[:, (c % per) * 128:(c % per + 1) * 128]
        if half == 64:
            partner = pltpu.roll(xb, 64, 1)
        else:
            lo = (lane % (2 * half)) < half
            partner = jnp.where(lo, pltpu.roll(xb, 128 - half, 1), pltpu.roll(xb, half, 1))
        outs.append(xb * cb + partner * sb)
    return jnp.concatenate(outs, axis=-1)


def _retention_scan(proj, cos, sin, dec, qd, kd, cd, nctx):
    B, NT, _ = proj.shape
    nc, ncc = NT // CH, nctx // CH
    scale = D_QK ** -0.5

    def kern(q_ref, k_ref, v_ref, cos_ref, sin_ref, dec_ref, qd_ref, kd_ref, cd_ref, o_ref, s_ref):
        @pl.when(pl.program_id(2) == 0)
        def _():
            s_ref[...] = jnp.zeros(s_ref.shape, F32)

        cos_t, sin_t = cos_ref[...], sin_ref[...]
        qd_t, kd_t = qd_ref[0], kd_ref[0]
        cd_t = cd_ref[0, 0:1, :]
        for hh in range(D_HEADS):
            qh = _rope_blocks(q_ref[0, :, hh * D_QK:(hh + 1) * D_QK], cos_t, sin_t, 64) * scale
            kh = _rope_blocks(k_ref[0, :, hh * D_QK:(hh + 1) * D_QK], cos_t, sin_t, 64)
            vh = v_ref[0, :, hh * D_V:(hh + 1) * D_V]
            s = _dot_nt(qh, kh) * dec_ref[0, hh]
            o_ref[0, :, hh * D_V:(hh + 1) * D_V] = _dot(s, vh) + _dot(qh, s_ref[hh]) * _col(qd_t, hh)
            s_ref[hh] = _col(cd_t, hh) * s_ref[hh] + _dot_tn(kh * _col(kd_t, hh), vh)

    ci = functools.partial(_chunk_index, ncc=ncc, nc=nc)
    WV = D_HEADS * D_V
    return pl.pallas_call(
        kern,
        grid=(B, 2, nc),
        in_specs=[pl.BlockSpec((1, CH, D), lambda b, d, c: (b, ci(d, c), 0)),
                  pl.BlockSpec((1, CH, D), lambda b, d, c: (b, ci(d, c), 1)),
                  pl.BlockSpec((1, CH, WV), lambda b, d, c: (b, ci(d, c), 1)),
                  pl.BlockSpec((CH, 256), lambda b, d, c: (ci(d, c), 0)),
                  pl.BlockSpec((CH, 256), lambda b, d, c: (ci(d, c), 0)),
                  pl.BlockSpec((1, D_HEADS, CH, CH), lambda b, d, c: (d, 0, 0, 0)),
                  pl.BlockSpec((1, CH, 128), lambda b, d, c: (d, 0, 0)),
                  pl.BlockSpec((1, CH, 128), lambda b, d, c: (d, 0, 0)),
                  pl.BlockSpec((1, 8, 128), lambda b, d, c: (d, 0, 0))],
        out_specs=pl.BlockSpec((1, CH, WV), lambda b, d, c: (b, ci(d, c), d)),
        out_shape=jax.ShapeDtypeStruct((B, NT, 2 * WV), F32),
        scratch_shapes=[pltpu.VMEM((D_HEADS, D_QK, D_V), F32)],
        compiler_params=_cp(("parallel", "arbitrary", "arbitrary")),
        name="retention_scan",
    )(proj, proj, proj, cos, sin, dec, qd, kd, cd)


def _ssd_scan(xbc, dtraw, dt_bias, a_rate, expand, tri, nctx):
    B, NT, _ = xbc.shape
    nc, ncc = NT // CH, nctx // CH
    R = B_HEADS // B_GROUPS
    GW = R * B_HD
    DI = B_HEADS * B_HD

    def kern(x_ref, b_ref, c_ref, dt_ref, bias_ref, a_ref, e_ref, tri_ref, o_ref, st_ref):
        @pl.when(pl.program_id(2) == 0)
        def _():
            st_ref[...] = jnp.zeros(st_ref.shape, F32)

        tri_m = tri_ref[0]
        mask = tri_m > 0.5
        E = e_ref[...]
        dt = _softplus(dt_ref[0] + bias_ref[0])
        dta = dt * a_ref[0]
        cum_c = _dot_hi(tri_m, dta)
        cum_r = _dot_nt_hi(dta.T, tri_m)
        a_last = jnp.sum(dta, axis=0, keepdims=True)
        xdt = x_ref[0] * _expand(dt, E)
        eacs = _expand(jnp.exp(cum_c), E)
        wst = _expand(jnp.exp(a_last - cum_c), E)
        da = _expand(jnp.broadcast_to(jnp.exp(a_last), (8, 128)), E)
        lane = lax.broadcasted_iota(jnp.int32, (CH, 128), 1)
        lo = lane < B_HD
        for g in range(B_GROUPS):
            bg = b_ref[0, :, g * B_STATE:(g + 1) * B_STATE]
            cg = c_ref[0, :, g * B_STATE:(g + 1) * B_STATE]
            cb = _dot_nt(cg, bg)
            xg = xdt[:, g * GW:(g + 1) * GW]
            y2 = _dot(cg, st_ref[g]) * eacs[:, g * GW:(g + 1) * GW]
            for pr in range(R // 2):
                xp = xg[:, pr * 128:(pr + 1) * 128]
                acc = y2[:, pr * 128:(pr + 1) * 128]
                for u in range(2):
                    hd = g * R + pr * 2 + u
                    seg = jnp.exp(jnp.where(mask, _col(cum_c, hd) - _row(cum_r, hd), NEG))
                    xm = jnp.where(lo if u == 0 else jnp.logical_not(lo), xp, 0.0)
                    acc = acc + _dot(cb * seg, xm)
                o_ref[0, :, g * GW + pr * 128:g * GW + (pr + 1) * 128] = acc
            st_ref[g] = da[0:1, g * GW:(g + 1) * GW] * st_ref[g] + _dot_tn(bg, xg * wst[:, g * GW:(g + 1) * GW])

    ci = functools.partial(_chunk_index, ncc=ncc, nc=nc)
    GS = B_GROUPS * B_STATE
    return pl.pallas_call(
        kern,
        grid=(B, 2, nc),
        in_specs=[pl.BlockSpec((1, CH, DI), lambda b, d, c: (b, ci(d, c), 0)),
                  pl.BlockSpec((1, CH, GS), lambda b, d, c: (b, ci(d, c), 2)),
                  pl.BlockSpec((1, CH, GS), lambda b, d, c: (b, ci(d, c), 3)),
                  pl.BlockSpec((1, CH, 128), lambda b, d, c: (b, ci(d, c), d)),
                  pl.BlockSpec((1, 1, 128), lambda b, d, c: (d, 0, 0)),
                  pl.BlockSpec((1, 1, 128), lambda b, d, c: (d, 0, 0)),
                  pl.BlockSpec((128, DI), lambda b, d, c: (0, 0)),
                  pl.BlockSpec((1, CH, CH), lambda b, d, c: (d, 0, 0))],
        out_specs=pl.BlockSpec((1, CH, DI), lambda b, d, c: (b, ci(d, c), d)),
        out_shape=jax.ShapeDtypeStruct((B, NT, 2 * DI), F32),
        scratch_shapes=[pltpu.VMEM((B_GROUPS, B_STATE, GW), F32)],
        compiler_params=_cp(("parallel", "arbitrary", "arbitrary")),
        name="ssd_scan",
    )(xbc, xbc, xbc, dtraw, dt_bias, a_rate, expand, tri)


def _attn_prep(qkv, cos, sin):
    B, NT, _ = qkv.shape
    scale = C_HD ** -0.5 * math.log2(math.e)

    def kern(x_ref, cos_ref, sin_ref, o_ref):
        j = pl.program_id(2)
        x = x_ref[0]

        @pl.when(j < 2)
        def _():
            y = _rope_blocks(x, cos_ref[...], sin_ref[...], C_HD // 4)
            o_ref[0] = (y * jnp.where(j == 0, scale, 1.0)).astype(BF)

        @pl.when(j == 2)
        def _():
            o_ref[0] = x.astype(BF)

    TR = 3 * RB
    assert NT % TR == 0
    return pl.pallas_call(
        kern,
        grid=(B, NT // TR, 3),
        in_specs=[pl.BlockSpec((1, TR, D), lambda b, i, j: (b, i, j)),
                  pl.BlockSpec((TR, 128), lambda b, i, j: (i, 0)),
                  pl.BlockSpec((TR, 128), lambda b, i, j: (i, 0))],
        out_specs=pl.BlockSpec((1, TR, D), lambda b, i, j: (b, i, j)),
        out_shape=jax.ShapeDtypeStruct((B, NT, 3 * D), BF),
        compiler_params=_cp(("parallel", "parallel", "arbitrary")),
        name="attn_prep",
    )(qkv, cos, sin)


def _diff_attention(qkvb, lam, norm_g, out_scale, nctx):
    B, NT, _ = qkvb.shape
    TQ = RB
    HP = 2 * C_HD

    def kern(q_ref, k_ref, v_ref, lam_ref, g_ref, o_ref):
        lam_v = lam_ref[0:1, 0:1]
        lane = lax.broadcasted_iota(jnp.int32, (TQ, HP), 1)
        lo = lane < C_HD

        def attend(nk):
            for hp in range(C_HEADS):
                qp = q_ref[0, :, hp * HP:(hp + 1) * HP]
                kp = k_ref[0, 0:nk, hp * HP:(hp + 1) * HP]
                vp = v_ref[0, 0:nk, hp * HP:(hp + 1) * HP]
                outs = []
                for u in range(2):
                    qm = jnp.where(lo if u == 0 else jnp.logical_not(lo), qp, jnp.zeros_like(qp))
                    s = lax.dot_general(qm, kp, (((1,), (1,)), ((), ())), preferred_element_type=F32)
                    p = jnp.exp2(s - jnp.max(s, axis=1, keepdims=True))
                    z = jnp.sum(p, axis=1, keepdims=True)
                    outs.append(jnp.dot(p.astype(BF), vp, preferred_element_type=F32) / z)
                o = outs[0] - lam_v * outs[1]
                o = o * lax.rsqrt(jnp.mean(o * o, axis=-1, keepdims=True) + RMS_EPS)
                o_ref[0, :, hp * HP:(hp + 1) * HP] = o * g_ref[...] * out_scale

        is_ctx = pl.program_id(1) * TQ < nctx

        @pl.when(is_ctx)
        def _():
            attend(nctx)

        @pl.when(jnp.logical_not(is_ctx))
        def _():
            attend(NT)

    return pl.pallas_call(
        kern,
        grid=(B, NT // TQ),
        in_specs=[pl.BlockSpec((1, TQ, D), lambda b, i: (b, i, 0)),
                  pl.BlockSpec((1, NT, D), lambda b, i: (b, 0, 1)),
                  pl.BlockSpec((1, NT, D), lambda b, i: (b, 0, 2)),
                  pl.BlockSpec((8, 128), lambda b, i: (0, 0)),
                  pl.BlockSpec((1, HP), lambda b, i: (0, 0))],
        out_specs=pl.BlockSpec((1, TQ, D), lambda b, i: (b, i, 0)),
        out_shape=jax.ShapeDtypeStruct((B, NT, D), F32),
        compiler_params=_cp(("parallel", "arbitrary")),
        name="diff_attention",
    )(qkvb, qkvb, qkvb, lam, norm_g.reshape(1, HP))


def _sort16_network():
    n, pairs, p = 16, [], 1
    while p < n:
        k = p
        while k >= 1:
            for j in range(k % p, n - k, 2 * k):
                for i in range(min(k, n - j - k)):
                    if (i + j) // (2 * p) == (i + j + k) // (2 * p):
                        pairs.append((i + j, i + j + k))
            k //= 2
        p *= 2
    return pairs


def _top16_desc(s):
    v = [s[8 * k:8 * (k + 1), :] for k in range(16)]
    for a, b in _sort16_network():
        v[a], v[b] = jnp.maximum(v[a], v[b]), jnp.minimum(v[a], v[b])
    tops = []
    for r in range(P_TOPK):
        m = jnp.max(v[0], axis=0, keepdims=True)
        tops.append(m)
        if r + 1 < P_TOPK:
            hit = v[0] == m
            for k in range(P_TOPK - 1 - r):
                v[k] = jnp.where(hit, v[k + 1], v[k])
    return tops


def _peer_ln(h, q16, mod, keys, u, vt, ln_g, ln_b, ts):
    B, NT, _ = h.shape
    TS, TE = ts, P_TE
    assert NT % TS == 0 and TS % RB == 0
    E = u.shape[0]
    ne = E // TE
    NTB = TS // 128
    NIB = TE // P_NK
    RPB = TS // RB
    hf = h.reshape(B * NT, D)
    modf = mod.reshape(B * (NT // RB), 6, D)
    pairs = [(ra, rb) for ra in range(P_TOPK) for rb in range(P_TOPK // (ra + 1))]
    NCAND = 8 * ((len(pairs) + 7) // 8)

    def kern(h_ref, q_ref, mod_ref, keys_ref, u_ref, vt_ref, g_ref, b_ref, o_ref,
             xin, rbk, e2m, lr, f1, cand, a_sc, w_sc, acc):
        e = pl.program_id(1)

        @pl.when(e == 0)
        def _():
            for r in range(RPB):
                hx = h_ref[r * RB:(r + 1) * RB, :]
                xin[:, r * RB:(r + 1) * RB] = (hx * (1.0 + mod_ref[r, 4:5, :]) + mod_ref[r, 3:4, :]).T.astype(BF)
            acc[...] = jnp.zeros(acc.shape, F32)

            def head_body(hh, carry):
                for tb in range(NTB):
                    tsl = slice(tb * 128, (tb + 1) * 128)
                    vals = []
                    for c in range(2):
                        s = lax.dot_general(keys_ref[2 * hh + c], q_ref[2 * hh + c, tsl, :],
                                            (((1,), (1,)), ((), ())), preferred_element_type=F32)
                        vals.append((s, _top16_desc(s)))
                    (s1, ta), (s2, tbv) = vals
                    cand[...] = jnp.full(cand.shape, NEG, F32)
                    for n, (pa, pb) in enumerate(pairs):
                        cand[n:n + 1, :] = ta[pa] + tbv[pb]
                    cv = cand[...]
                    work = cv
                    for r in range(P_TOPK):
                        th = jnp.max(work, axis=0, keepdims=True)
                        if r + 1 < P_TOPK:
                            work = jnp.where(work == th, NEG, work)
                    top = ta[0] + tbv[0]
                    z = jnp.sum(jnp.where(cv >= th, jnp.exp(cv - top), 0.0), axis=0, keepdims=True)
                    lrv = jnp.zeros(s1.shape, F32)
                    rb = jnp.zeros(s2.shape, F32)
                    for pa in range(P_TOPK):
                        cnt = jnp.zeros(th.shape, F32)
                        for pb in range(P_TOPK // (pa + 1)):
                            cnt = cnt + jnp.where(ta[pa] + tbv[pb] >= th, 1.0, 0.0)
                        lrv = jnp.where(s1 == ta[pa], cnt, lrv)
                        rb = jnp.where(s2 < tbv[pa], float(pa + 1), rb)
                    lr[hh, :, tsl] = lrv
                    f1[hh, :, tsl] = jnp.where(s1 >= ta[-1], 0.5 * jnp.exp(s1 - ta[0]), 0.0)
                    rbk[hh, :, tsl] = rb.astype(BF)
                    e2m[hh, :, tsl] = (jnp.where(s2 >= tbv[-1], jnp.exp(s2 - tbv[0]), 0.0) / z).astype(BF)
                return carry

            lax.fori_loop(0, P_HEADS, head_body, 0)

        a_sc[...] = jnp.dot(u_ref[...], xin[...], preferred_element_type=F32)
        i8 = pl.multiple_of(e * NIB, 8)
        zero = jnp.zeros((), BF)
        for tb in range(NTB):
            tsl = slice(tb * 128, (tb + 1) * 128)
            lrb = [lr[hh, pl.ds(i8, NIB), tsl] for hh in range(P_HEADS)]
            f1b = [f1[hh, pl.ds(i8, NIB), tsl] for hh in range(P_HEADS)]
            for k in range(NIB):
                gm = None
                for hh in range(P_HEADS):
                    sel = jnp.where(rbk[hh, :, tsl] < lrb[hh][k:k + 1, :].astype(BF), e2m[hh, :, tsl], zero)
                    term = sel * f1b[hh][k:k + 1, :].astype(BF)
                    gm = term if gm is None else gm + term
                ksl = slice(k * P_NK, (k + 1) * P_NK)
                w_sc[ksl, tsl] = gm * _gelu_x2(a_sc[ksl, tsl].astype(BF))
        acc[...] += jnp.dot(vt_ref[0], w_sc[...], preferred_element_type=F32)

        @pl.when(e == ne - 1)
        def _():
            y = acc[...].T
            for r in range(RPB):
                hx = h_ref[r * RB:(r + 1) * RB, :]
                o_ref[r * RB:(r + 1) * RB, :] = _layer_norm(
                    ALPHA * hx + mod_ref[r, 5:6, :] * y[r * RB:(r + 1) * RB, :], g_ref[...], b_ref[...])

    out = pl.pallas_call(
        kern,
        grid=(B * NT // TS, ne),
        in_specs=[pl.BlockSpec((TS, D), lambda s, e: (s, 0)),
                  pl.BlockSpec((2 * P_HEADS, TS, P_NK), lambda s, e: (0, s, 0)),
                  pl.BlockSpec((RPB, 6, D), lambda s, e: (s, 0, 0)),
                  pl.BlockSpec((2 * P_HEADS, P_NK, P_NK), lambda s, e: (0, 0, 0)),
                  pl.BlockSpec((TE, D), lambda s, e: (e, 0)),
                  pl.BlockSpec((1, D, TE), lambda s, e: (e, 0, 0)),
                  pl.BlockSpec((1, D), lambda s, e: (0, 0)),
                  pl.BlockSpec((1, D), lambda s, e: (0, 0))],
        out_specs=pl.BlockSpec((TS, D), lambda s, e: (s, 0)),
        out_shape=jax.ShapeDtypeStruct((B * NT, D), F32),
        scratch_shapes=[pltpu.VMEM((D, TS), BF),
                        pltpu.VMEM((P_HEADS, P_NK, TS), BF), pltpu.VMEM((P_HEADS, P_NK, TS), BF),
                        pltpu.VMEM((P_HEADS, P_NK, TS), F32), pltpu.VMEM((P_HEADS, P_NK, TS), F32),
                        pltpu.VMEM((NCAND, 128), F32),
                        pltpu.VMEM((TE, TS), F32), pltpu.VMEM((TE, TS), BF),
                        pltpu.VMEM((D, TS), F32)],
        compiler_params=_cp(("parallel", "arbitrary")),
        name="peer",
    )(hf, q16, modf, keys, u, vt, ln_g.reshape(1, D), ln_b.reshape(1, D))
    return out.reshape(B, NT, D)


def _rope_table(nctx, nlat, half, reps):
    t = jnp.arange(nlat, dtype=jnp.int32)
    freqs = ROPE_BASE ** (-jnp.arange(half, dtype=F32) / half)
    sign = jnp.concatenate([-jnp.ones((half,), F32), jnp.ones((half,), F32)])
    cs, sn = [], []
    for pos in (t // GRID_W, t % GRID_W):
        ang = pos.astype(F32)[:, None] * freqs
        cs.append(jnp.concatenate([jnp.cos(ang), jnp.cos(ang)], axis=-1))
        sn.append(jnp.concatenate([jnp.sin(ang), jnp.sin(ang)], axis=-1) * sign)
    cos = jnp.tile(jnp.concatenate(cs, axis=-1), (1, reps))
    sin = jnp.tile(jnp.concatenate(sn, axis=-1), (1, reps))
    cos = jnp.concatenate([jnp.ones((nctx, cos.shape[1]), F32), cos], axis=0)
    sin = jnp.concatenate([jnp.zeros((nctx, sin.shape[1]), F32), sin], axis=0)
    return cos, sin


def _retention_tables(decay_logit):
    lg = jax.nn.log_sigmoid(decay_logit.astype(F32))
    pos = jnp.arange(CH, dtype=F32)
    rel = pos[:, None] - pos[None, :]
    lower = jnp.where(rel >= 0, jnp.exp(jnp.maximum(rel, 0.0) * lg[:, :, None, None]), 0.0)
    dec = jnp.stack([lower[0], jnp.swapaxes(lower[1], -1, -2)])
    qd_f = jnp.exp((pos + 1.0)[:, None] * lg[0][None, :])
    kd_f = jnp.exp((CH - 1.0 - pos)[:, None] * lg[0][None, :])
    qd_b = jnp.exp((CH - pos)[:, None] * lg[1][None, :])
    kd_b = jnp.exp(pos[:, None] * lg[1][None, :])
    pad = lambda a: jnp.pad(a, ((0, 0), (0, 128 - a.shape[1])))
    qd = jnp.stack([pad(qd_f), pad(qd_b)])
    kd = jnp.stack([pad(kd_f), pad(kd_b)])
    cd = jnp.broadcast_to(jnp.pad(jnp.exp(CH * lg), ((0, 0), (0, 128 - D_HEADS)))[:, None, :], (2, 8, 128))
    return dec, qd, kd, cd


def _mlstm_layer(h, mod, lnc, tri, nctx, w_up, conv_w, conv_b, w_qk, w_v, w_gate, b_gate, norm_g, skip, w_down):
    DI = A_HEADS * A_HD
    up = _inproj(h, mod, w_up.astype(BF), name="mlstm_up")
    xc = _conv_silu(up, 0, DI // 256, conv_w, conv_b, nctx)
    qk = _rowmm([(xc, DI, 0)], [], w_qk.astype(BF), _pro_first, out_dtype=F32, name="mlstm_qk")
    v = _rowmm([(up, DI, 0)], [], w_v.astype(BF), _pro_first, out_dtype=F32, name="mlstm_v")
    wg = w_gate.reshape(3 * DI, 4, A_HEADS)
    bg = b_gate.reshape(4, A_HEADS)
    zw = jnp.zeros((3 * DI, 128 - 2 * A_HEADS), F32)
    zb = jnp.zeros((128 - 2 * A_HEADS,), F32)
    wg2 = jnp.concatenate([wg[:, 0], wg[:, 1], zw, wg[:, 2], wg[:, 3], zw], axis=1).astype(BF)
    bg2 = jnp.concatenate([bg[0], bg[1], zb, bg[2], bg[3], zb]).reshape(1, 256)

    def gate_epi(y, rv, cv, mv):
        return y + _dot(rv[1], cv[0]) + _dot(rv[2], cv[1]) + cv[2]

    gates = _rowmm([(v, DI, 0), (qk, DI, 0), (qk, DI, 1)], [wg2[:DI], wg2[DI:2 * DI], bg2], wg2[2 * DI:],
                   _pro_first, gate_epi, out_dtype=F32, name="mlstm_gates")
    hs = _mlstm_scan(qk, v, gates, tri, nctx)

    def out_pro(rv, cv, mv):
        hf, hb, z, o_pre, xcv, _ = rv
        hh = _sigmoid(o_pre) * (hf + hb)
        return (_head_norm(hh, A_HEADS, A_HD) * cv[0] + cv[1] * xcv) * _silu(z)

    return _rowmm([(hs, DI, 0), (hs, DI, 1), (up, DI, 1), (up, DI, 2), (xc, DI, 0), (h, D, 0)],
                  [norm_g.reshape(1, DI), skip.reshape(1, DI)] + lnc, w_down.astype(BF), out_pro, _epi_ln,
                  out_dtype=F32, mod=mod, name="mlstm_out")


def _ssd_layer(h, mod, lnc, tri, nctx, w_in, conv_w, conv_b, dt_bias, a_log, d_skip, norm_g, w_out):
    DI = B_HEADS * B_HD
    CC = DI + 2 * B_GROUPS * B_STATE
    proj = _inproj(h, mod, w_in[:, :DI + CC].astype(BF), name="ssd_in")
    wdt = w_in[:, DI + CC:].reshape(D, 2, B_HEADS)
    zpad = jnp.zeros((D, 128 - B_HEADS), F32)
    wdt2 = jnp.concatenate([wdt[:, 0], zpad, wdt[:, 1], zpad], axis=1).astype(BF)
    dtraw = _inproj(h, mod, wdt2, name="ssd_dt")
    xbc = _conv_silu(proj, DI // 256, CC // 256, conv_w, conv_b, nctx)
    padl = lambda a: jnp.pad(a.astype(F32), ((0, 0), (0, 128 - B_HEADS))).reshape(2, 1, 128)
    expand = (jnp.arange(128)[:, None] == (jnp.arange(DI)[None, :] // B_HD)).astype(BF)
    ys = _ssd_scan(xbc, dtraw, padl(dt_bias), padl(-jnp.exp(a_log.astype(F32))), expand, tri, nctx)
    dvec = jnp.repeat(d_skip.astype(F32), B_HD).reshape(1, DI)
    GW = DI // B_GROUPS

    def out_pro(rv, cv, mv):
        yf, yb, xs, z, _ = rv
        y = (yf + yb + cv[0] * xs) * _silu(z)
        outs = []
        for g in range(B_GROUPS):
            yg = y[:, g * GW:(g + 1) * GW]
            outs.append(yg * lax.rsqrt(jnp.mean(yg * yg, axis=-1, keepdims=True) + RMS_EPS))
        return jnp.concatenate(outs, axis=-1) * cv[1]

    return _rowmm([(ys, DI, 0), (ys, DI, 1), (xbc, DI, 0), (proj, DI, 0), (h, D, 0)],
                  [dvec, norm_g.reshape(1, DI)] + lnc, w_out.astype(BF), out_pro, _epi_ln,
                  out_dtype=F32, mod=mod, name="ssd_out")


def _diff_layer(h, mod, lnc, nctx, layer_idx, w_qkv, lam_vecs, norm_g, w_out):
    B, NT, _ = h.shape
    lam_init = 0.8 - 0.6 * math.exp(-0.3 * layer_idx)
    lv = lam_vecs.astype(F32)
    lam = jnp.exp(jnp.sum(lv[0] * lv[1])) - jnp.exp(jnp.sum(lv[2] * lv[3])) + lam_init
    qkv = _inproj(h, mod, w_qkv.astype(BF), name="diff_qkv")
    cos, sin = _rope_table(nctx, NT - nctx, C_HD // 4, 2)
    qkvb = _attn_prep(qkv, cos, sin)
    o = _diff_attention(qkvb, jnp.full((8, 128), lam, F32), norm_g, 1.0 - lam_init, nctx)
    return _rowmm([(o, D, 0), (h, D, 0)], lnc, w_out.astype(BF), _pro_first, _epi_ln, out_dtype=F32, mod=mod,
                  name="diff_out")


def _retention_layer(h, mod, lnc, nctx, w_in, decay_logit, norm_g, w_out):
    B, NT, _ = h.shape
    DV = D_HEADS * D_V
    proj = _inproj(h, mod, w_in.astype(BF), name="ret_in")
    cos, sin = _rope_table(nctx, NT - nctx, D_QK // 4, 1)
    dec, qd, kd, cd = _retention_tables(decay_logit)
    os_ = _retention_scan(proj, cos, sin, dec, qd, kd, cd, nctx)

    def out_pro(rv, cv, mv):
        of, ob, g, _ = rv
        return _head_norm(of + ob, D_HEADS, D_V) * cv[0] * _silu(g)

    return _rowmm([(os_, DV, 0), (os_, DV, 1), (proj, DV, 2), (h, D, 0)], [norm_g.reshape(1, DV)] + lnc,
                  w_out.astype(BF), out_pro, _epi_ln, out_dtype=F32, mod=mod, name="ret_out")


def kernel(x, c, ctx, c_ctx, ada_w, ada_b, ln_g, ln_b, peer_wq, peer_keys, peer_u, peer_v, mlstm_w_up, mlstm_conv_w, mlstm_conv_b, mlstm_w_qk, mlstm_w_v, mlstm_w_gate, mlstm_b_gate, mlstm_norm_g, mlstm_skip, mlstm_w_down, ssd_w_in, ssd_conv_w, ssd_conv_b, ssd_dt_bias, ssd_a_log, ssd_d, ssd_norm_g, ssd_w_out, diff_w_qkv, diff_lambda, diff_norm_g, diff_w_out, ret_w_in, ret_decay_logit, ret_norm_g, ret_w_out):
    B, nlat, _ = x.shape
    nctx = ctx.shape[1]
    NT = nctx + nlat
    nrb = NT // RB
    h = jnp.concatenate([ctx, x], axis=1)
    rows = 8 * ((B + 1 + 7) // 8)
    cond = jnp.concatenate([c, c_ctx[None, :], jnp.zeros((rows - B - 1, D), F32)], axis=0)
    modall = _modulation(cond, ada_w, ada_b)
    tri = _tri_masks()
    for i in range(DEPTH):
        kind, j = i % 4, i // 4
        ml = modall[i, :B].reshape(B, 1, 6, D)
        mc = modall[i, B].reshape(1, 1, 6, D)
        mod = jnp.concatenate([jnp.broadcast_to(mc, (B, nctx // RB, 6, D)),
                               jnp.broadcast_to(ml, (B, nlat // RB, 6, D))], axis=1).reshape(B, nrb, 6, D)
        lnc = [ln_g[i, 0].reshape(1, D), ln_b[i, 0].reshape(1, D)]
        if kind == 0:
            h = _mlstm_layer(h, mod, lnc, tri, nctx, mlstm_w_up[j], mlstm_conv_w[j], mlstm_conv_b[j], mlstm_w_qk[j],
                             mlstm_w_v[j], mlstm_w_gate[j], mlstm_b_gate[j], mlstm_norm_g[j], mlstm_skip[j],
                             mlstm_w_down[j])
        elif kind == 1:
            h = _ssd_layer(h, mod, lnc, tri, nctx, ssd_w_in[j], ssd_conv_w[j], ssd_conv_b[j], ssd_dt_bias[j],
                           ssd_a_log[j], ssd_d[j], ssd_norm_g[j], ssd_w_out[j])
        elif kind == 2:
            h = _diff_layer(h, mod, lnc, nctx, i, diff_w_qkv[j], diff_lambda[j], diff_norm_g[j], diff_w_out[j])
        else:
            h = _retention_layer(h, mod, lnc, nctx, ret_w_in[j], ret_decay_logit[j], ret_norm_g[j], ret_w_out[j])
        ts = P_TS
        if i == DEPTH - 1:
            h, mod, ts = h[:, nctx:, :], mod[:, nctx // RB:], P_TS_LAST
        q16 = _inproj(h, mod, peer_wq[i].astype(BF), out_dtype=BF, ffn=True, split_out=True,
                      name="peer_q").reshape(2 * P_HEADS, -1, P_NK)
        keys = peer_keys[i].reshape(2 * P_HEADS, P_NK, P_NK).astype(BF)
        vt = jnp.swapaxes(peer_v[i].astype(BF).reshape(-1, P_TE, D), 1, 2)
        h = _peer_ln(h, q16, mod, keys, peer_u[i].astype(BF), vt, ln_g[i, 1], ln_b[i, 1], ts)
    return h
```
